```python
import math
import jax, jax.numpy as jnp
from jax import lax
import numpy as np

D_MODEL = 1024
BATCH = 32
SEQ = 2048
DEPTH = 4

GRID_W = 64
CTX_LEN = 256
N_MOD = 9
D_FF = 2816
CONV_CH = D_MODEL
CONV_K = 31
DN_DK = 128
DN_DV = 128
DN_HEADS = D_MODEL // 128
DN_HK = DN_HEADS * DN_DK
DN_HV = DN_HEADS * DN_DV
SHORT_K = 3
CHUNK = 64
EPS = 1e-6

O_CONV = 0
O_QKV = O_CONV + 2 * CONV_CH
O_Z = O_QKV + 2 * DN_HK + DN_HV
O_AB = O_Z + DN_HV
O_GATE = O_AB + 4 * DN_HEADS
N_IN = O_GATE + 2 * D_MODEL

kernel_name = "hybrid_conformer_gdn_prefix_dit"


def _rms(x, g):
    xf = x.astype(jnp.float32)
    y = xf * lax.rsqrt(jnp.mean(xf * xf, axis=-1, keepdims=True) + EPS)
    return y.astype(x.dtype) * g


def _layernorm(x, g, b):
    xf = x.astype(jnp.float32)
    mu = jnp.mean(xf, axis=-1, keepdims=True)
    var = jnp.mean(jnp.square(xf - mu), axis=-1, keepdims=True)
    return ((xf - mu) * lax.rsqrt(var + EPS)).astype(x.dtype) * g + b


def _l2norm(x):
    xf = x.astype(jnp.float32)
    return xf * lax.rsqrt(jnp.sum(xf * xf, axis=-1, keepdims=True) + EPS)


def _modulate(n, shift, scale):
    return n * (1.0 + scale) + shift


def _swiglu(x, w13, w2):
    a, b = jnp.split(x @ w13, 2, axis=-1)
    return (jax.nn.silu(a) * b) @ w2


def _dwconv(x, w):
    k = w.shape[0]
    return lax.conv_general_dilated(
        x, w[:, None, :].astype(x.dtype), window_strides=(1,), padding=[(k // 2, k // 2)],
        dimension_numbers=("NWC", "WIO", "NWC"), feature_group_count=x.shape[-1])


def _conformer_conv(u, rows, dw, dw_b, ln_g, ln_b, proj):
    a, gate = jnp.split(u, 2, axis=-1)
    y = a * jax.nn.sigmoid(gate)
    b, t, ch = y.shape
    y = _dwconv(y.reshape(b * rows, t // rows, ch), dw).reshape(b, t, ch) + dw_b
    y = jax.nn.silu(_layernorm(y, ln_g, ln_b))
    return y @ proj


def _dn_inputs(p_qkv, p_ab, short_w, a_log, dt_bias):
    b, t, _ = p_qkv.shape
    qkv = jax.nn.silu(_dwconv(p_qkv, short_w))
    q, k, v = jnp.split(qkv, [DN_HK, 2 * DN_HK], axis=-1)
    q = _l2norm(q.reshape(b, t, DN_HEADS, DN_DK)) * (DN_DK ** -0.5)
    k = _l2norm(k.reshape(b, t, DN_HEADS, DN_DK))
    v = v.reshape(b, t, DN_HEADS, DN_DV).astype(jnp.float32)
    ab = p_ab.astype(jnp.float32).reshape(b, t, 2, 2, DN_HEADS)
    g = -jnp.exp(a_log.astype(jnp.float32)) * jax.nn.softplus(ab[:, :, 0] + dt_bias.astype(jnp.float32))
    beta = jax.nn.sigmoid(ab[:, :, 1])
    return q, k, v, g, beta


def _chunk_gdr(q, k, v, g, beta, s0):
    b, t, h, dk = q.shape
    dv = v.shape[-1]
    n = t // CHUNK

    def blocks(a):
        return jnp.moveaxis(a.reshape(b, n, CHUNK, h, *a.shape[3:]), 2, 3)

    q, k, v, g, beta = (blocks(a) for a in (q, k, v, g, beta))
    g = jnp.cumsum(g, axis=-1)
    idx = jnp.arange(CHUNK)
    incl = idx[:, None] >= idx[None, :]
    strict = idx[:, None] > idx[None, :]
    decay = jnp.exp(jnp.where(incl, g[..., :, None] - g[..., None, :], -jnp.inf))
    kb = k * beta[..., None]
    a_mat = jnp.einsum("bnhid,bnhjd->bnhij", kb, k) * decay
    lower = jnp.where(strict, a_mat, 0.0) + jnp.eye(CHUNK, dtype=a_mat.dtype)
    rhs = jnp.concatenate([v * beta[..., None], kb * jnp.exp(g)[..., None]], axis=-1)
    sol = lax.linalg.triangular_solve(lower, rhs, left_side=True, lower=True, unit_diagonal=True)
    u, w = sol[..., :dv], sol[..., dv:]
    qk = jnp.einsum("bnhid,bnhjd->bnhij", q, k) * decay

    def step(s, blk):
        qb, kbk, ub, wb, gb, qkb = blk
        v_new = ub - jnp.einsum("bhck,bhkv->bhcv", wb, s)
        o = (jnp.einsum("bhck,bhkv->bhcv", qb * jnp.exp(gb)[..., None], s)
             + jnp.einsum("bhij,bhjv->bhiv", qkb, v_new))
        g_last = gb[..., -1:]
        s = (s * jnp.exp(g_last)[..., None]
             + jnp.einsum("bhck,bhcv->bhkv", kbk * jnp.exp(g_last - gb)[..., None], v_new))
        return s, o

    xs = tuple(jnp.moveaxis(a, 1, 0) for a in (q, k, u, w, g, qk))
    s_final, o = lax.scan(step, s0, xs)
    o = jnp.moveaxis(jnp.moveaxis(o, 0, 1), 2, 3).reshape(b, t, h, dv)
    return o, s_final


def _scan_dir(q, k, v, g, beta, s0, reverse):
    if reverse:
        q, k, v, g, beta = (jnp.flip(a, axis=1) for a in (q, k, v, g, beta))
    o, s = _chunk_gdr(q, k, v, g, beta, s0)
    return (jnp.flip(o, axis=1) if reverse else o), s


def _merge_out(p, o, rows, conv_dw, conv_dw_b, conv_ln_g, conv_ln_b, conv_proj, dn_onorm, dn_proj, w_out):
    b, t, _ = p.shape
    y_conv = _conformer_conv(p[..., O_CONV:O_QKV], rows, conv_dw, conv_dw_b, conv_ln_g, conv_ln_b, conv_proj)
    z = p[..., O_Z:O_AB].reshape(b, t, DN_HEADS, DN_DV)
    o = _rms(o.astype(p.dtype), dn_onorm) * jax.nn.silu(z)
    y_dn = o.reshape(b, t, DN_HV) @ dn_proj
    g_conv, g_dn = jnp.split(p[..., O_GATE:N_IN], 2, axis=-1)
    return (jax.nn.sigmoid(g_conv) * y_conv + jax.nn.sigmoid(g_dn) * y_dn) @ w_out


def _mixer(n_lat, n_ctx, rows, w_in, conv_dw, conv_dw_b, conv_ln_g, conv_ln_b, conv_proj,
           dn_short, dn_a_log, dn_dt_bias, dn_onorm, dn_proj, w_out, need_ctx_out):
    p_lat = n_lat @ w_in
    p_ctx = n_ctx @ w_in
    ql, kl, vl, gl, bl = _dn_inputs(p_lat[..., O_QKV:O_Z], p_lat[..., O_AB:O_GATE], dn_short, dn_a_log, dn_dt_bias)
    qc, kc, vc, gc, bc = _dn_inputs(p_ctx[..., O_QKV:O_Z], p_ctx[..., O_AB:O_GATE], dn_short, dn_a_log, dn_dt_bias)
    s0 = jnp.zeros((n_ctx.shape[0], DN_HEADS, DN_DK, DN_DV), jnp.float32)
    o_lat = 0.0
    o_ctx = 0.0
    for d, rev in enumerate((False, True)):
        oc, s_ctx = _scan_dir(qc, kc, vc, gc[:, :, d], bc[:, :, d], s0, rev)
        ol, _ = _scan_dir(ql, kl, vl, gl[:, :, d], bl[:, :, d], s_ctx, rev)
        o_lat = o_lat + ol
        o_ctx = o_ctx + oc
    y_lat = _merge_out(p_lat, o_lat, rows, conv_dw, conv_dw_b, conv_ln_g, conv_ln_b, conv_proj, dn_onorm, dn_proj, w_out)
    if not need_ctx_out:
        return y_lat, None
    y_ctx = _merge_out(p_ctx, o_ctx, 1, conv_dw, conv_dw_b, conv_ln_g, conv_ln_b, conv_proj, dn_onorm, dn_proj, w_out)
    return y_lat, y_ctx


def setup_inputs(seed: int = 0) -> dict:
    key = jax.random.key(seed)
    ks = jax.random.split(key, 32)

    def nrm(k, shape, scale=1.0):
        return scale * jax.random.normal(k, shape, jnp.float32)

    dt = jnp.exp(jax.random.uniform(ks[20], (DEPTH, 2, DN_HEADS), jnp.float32, math.log(1e-3), math.log(1e-1)))
    return {
        "x": nrm(ks[0], (BATCH, SEQ, D_MODEL)),
        "c": nrm(ks[1], (BATCH, D_MODEL)),
        "ctx": nrm(ks[2], (BATCH, CTX_LEN, D_MODEL)),
        "c_ctx": nrm(ks[3], (D_MODEL,)),
        "ada_w": nrm(ks[4], (DEPTH, D_MODEL, N_MOD * D_MODEL), 0.5 * D_MODEL ** -0.5),
        "ada_b": nrm(ks[5], (DEPTH, N_MOD * D_MODEL), 0.02),
        "ffn1_norm": 1.0 + nrm(ks[6], (DEPTH, D_MODEL), 0.1),
        "ffn1_w13": nrm(ks[7], (DEPTH, D_MODEL, 2 * D_FF), D_MODEL ** -0.5),
        "ffn1_w2": nrm(ks[8], (DEPTH, D_FF, D_MODEL), D_FF ** -0.5),
        "mix_norm": 1.0 + nrm(ks[9], (DEPTH, D_MODEL), 0.1),
        "w_in": nrm(ks[10], (DEPTH, D_MODEL, N_IN), D_MODEL ** -0.5),
        "conv_dw": nrm(ks[11], (DEPTH, CONV_K, CONV_CH), CONV_K ** -0.5),
        "conv_dw_b": nrm(ks[12], (DEPTH, CONV_CH), 0.02),
        "conv_ln_g": 1.0 + nrm(ks[13], (DEPTH, CONV_CH), 0.1),
        "conv_ln_b": nrm(ks[14], (DEPTH, CONV_CH), 0.02),
        "conv_proj": nrm(ks[15], (DEPTH, CONV_CH, D_MODEL), CONV_CH ** -0.5),
        "dn_short": nrm(ks[16], (DEPTH, SHORT_K, 2 * DN_HK + DN_HV), SHORT_K ** -0.5),
        "dn_a_log": jnp.log(jax.random.uniform(ks[17], (DEPTH, 2, DN_HEADS), jnp.float32, 1.0, 16.0)),
        "dn_dt_bias": dt + jnp.log(-jnp.expm1(-dt)),
        "dn_onorm": 1.0 + nrm(ks[18], (DEPTH, DN_DV), 0.1),
        "dn_proj": nrm(ks[19], (DEPTH, DN_HV, D_MODEL), DN_HV ** -0.5),
        "w_out": nrm(ks[21], (DEPTH, D_MODEL, D_MODEL), D_MODEL ** -0.5),
        "ffn2_norm": 1.0 + nrm(ks[22], (DEPTH, D_MODEL), 0.1),
        "ffn2_w13": nrm(ks[23], (DEPTH, D_MODEL, 2 * D_FF), D_MODEL ** -0.5),
        "ffn2_w2": nrm(ks[24], (DEPTH, D_FF, D_MODEL), D_FF ** -0.5),
        "final_norm": 1.0 + nrm(ks[25], (D_MODEL,), 0.1),
    }


def reference(x, c, ctx, c_ctx, ada_w, ada_b, ffn1_norm, ffn1_w13, ffn1_w2, mix_norm, w_in,
              conv_dw, conv_dw_b, conv_ln_g, conv_ln_b, conv_proj, dn_short, dn_a_log, dn_dt_bias,
              dn_onorm, dn_proj, w_out, ffn2_norm, ffn2_w13, ffn2_w2, final_norm):
    rows = x.shape[1] // GRID_W
    h = x
    hc = ctx
    for l in range(DEPTH):
        last = l == DEPTH - 1
        ml = jnp.split((jax.nn.silu(c) @ ada_w[l] + ada_b[l])[:, None, :], N_MOD, axis=-1)
        mc = jnp.split(jax.nn.silu(c_ctx) @ ada_w[l] + ada_b[l], N_MOD, axis=-1)
        h = h + 0.5 * ml[2] * _swiglu(_modulate(_rms(h, ffn1_norm[l]), ml[0], ml[1]), ffn1_w13[l], ffn1_w2[l])
        hc = hc + 0.5 * mc[2] * _swiglu(_modulate(_rms(hc, ffn1_norm[l]), mc[0], mc[1]), ffn1_w13[l], ffn1_w2[l])
        y_lat, y_ctx = _mixer(
            _modulate(_rms(h, mix_norm[l]), ml[3], ml[4]), _modulate(_rms(hc, mix_norm[l]), mc[3], mc[4]), rows,
            w_in[l], conv_dw[l], conv_dw_b[l], conv_ln_g[l], conv_ln_b[l], conv_proj[l],
            dn_short[l], dn_a_log[l], dn_dt_bias[l], dn_onorm[l], dn_proj[l], w_out[l], not last)
        h = h + ml[5] * y_lat
        h = h + 0.5 * ml[8] * _swiglu(_modulate(_rms(h, ffn2_norm[l]), ml[6], ml[7]), ffn2_w13[l], ffn2_w2[l])
        if not last:
            hc = hc + mc[5] * y_ctx
            hc = hc + 0.5 * mc[8] * _swiglu(_modulate(_rms(hc, ffn2_norm[l]), mc[6], mc[7]), ffn2_w13[l], ffn2_w2[l])
    return _rms(h, final_norm)
```

```python
import functools

import jax
import jax.numpy as jnp
from jax import lax
from jax.experimental import pallas as pl
from jax.experimental.pallas import tpu as pltpu

F32 = jnp.float32
BF16 = jnp.bfloat16

EPS = 1e-6
D = 1024
DFF = 2816
NMOD = 9
GRID_W = 64
CONV_K = 31
CONV_PAD = 16
SHORT_K = 3
H = 8
DK = 128
CHUNK = 64
DN_BLOCK = 256
HALO = 16
MOD_ROWS = 40

P_CONV = 0
P_QKV = P_CONV + 2 * D
P_Z = P_QKV + 3 * D
P_GATE = P_Z + D
P_AB = P_GATE + 2 * D
AB_PAD = 128
P_END = P_AB + AB_PAD

VMEM_LIMIT = 56 * 1024 * 1024


def _cparams(*sem):
    return pltpu.CompilerParams(dimension_semantics=sem, vmem_limit_bytes=VMEM_LIMIT)


def _resident(shape):
    nd = len(shape)
    return pl.BlockSpec(shape, lambda *_: (0,) * nd, pipeline_mode=pl.Buffered(1))


def _mod_spec(layer, part, row_fn):
    return pl.BlockSpec((None, None, 1, 3 * D), lambda i: (layer, row_fn(i), 0, part))


def _sigmoid(x):
    return jax.nn.sigmoid(x)


def _rms_mod(h, nw, shift, scale):
    ms = jnp.mean(h * h, axis=-1, keepdims=True)
    n = h * lax.rsqrt(ms + EPS) * nw
    return n * (1.0 + scale) + shift


def _ada_kernel(c_ref, w_ref, b_ref, o_ref):
    c = c_ref[...]
    sc = c * _sigmoid(c)
    o_ref[...] = jnp.dot(sc, w_ref[...], preferred_element_type=F32,
                         precision=lax.Precision.HIGHEST) + b_ref[...]


def _ada(c_all, ada_w, ada_b):
    depth = ada_w.shape[0]
    rows = c_all.shape[0]
    return pl.pallas_call(
        _ada_kernel,
        grid=(depth, NMOD),
        in_specs=[pl.BlockSpec((rows, D), lambda l, j: (0, 0)),
                  pl.BlockSpec((None, D, D), lambda l, j: (l, 0, j)),
                  pl.BlockSpec((None, 1, D), lambda l, j: (l, 0, j))],
        out_specs=pl.BlockSpec((None, rows, D), lambda l, j: (l, 0, j)),
        out_shape=jax.ShapeDtypeStruct((depth, rows, NMOD * D), F32),
        compiler_params=_cparams("parallel", "parallel"),
        name="ada",
    )(c_all, ada_w, ada_b.reshape(depth, 1, NMOD * D))


def _ffn_kernel(h_ref, mod_ref, nw_ref, w13_ref, w2_ref, o_ref):
    h = h_ref[...]
    mod = mod_ref[...]
    n = _rms_mod(h, nw_ref[...], mod[:, :D], mod[:, D:2 * D])
    ab = jnp.dot(n.astype(BF16), w13_ref[...], preferred_element_type=F32)
    a = ab[:, :DFF]
    b = ab[:, DFF:]
    s = (a * _sigmoid(a) * b).astype(BF16)
    y = jnp.dot(s, w2_ref[...], preferred_element_type=F32)
    o_ref[...] = h + (0.5 * mod[:, 2 * D:]) * y


def _ffn(h, mods, layer, part, row_fn, nw, w13, w2, tm):
    n_tok = h.shape[0]
    return pl.pallas_call(
        _ffn_kernel,
        grid=(n_tok // tm,),
        in_specs=[pl.BlockSpec((tm, D), lambda i: (i, 0)),
                  _mod_spec(layer, part, row_fn),
                  _resident((1, D)),
                  _resident((D, 2 * DFF)),
                  _resident((DFF, D))],
        out_specs=pl.BlockSpec((tm, D), lambda i: (i, 0)),
        out_shape=jax.ShapeDtypeStruct((n_tok, D), F32),
        compiler_params=_cparams("parallel"),
        name="ffn",
    )(h, mods, nw, w13, w2)


def _mixin_kernel(h_ref, mod_ref, nw_ref, win_ref, dw_ref, dwb_ref, lng_ref, lnb_ref, cproj_ref,
                  ma_ref, qkv_ref, z_ref, gd_ref, ab_ref, pad_ref, *, row_len):
    tm = h_ref.shape[0]
    n_rows = tm // row_len
    stride = row_len + 2 * CONV_PAD
    mod = mod_ref[...]
    n = _rms_mod(h_ref[...], nw_ref[...], mod[:, :D], mod[:, D:2 * D]).astype(BF16)

    u = jnp.dot(n, win_ref[:, P_CONV:P_QKV], preferred_element_type=F32)
    y = u[:, :D] * _sigmoid(u[:, D:])
    zeros = jnp.zeros((CONV_PAD, D), F32)
    for r in range(n_rows):
        base = r * stride
        pad_ref[base:base + CONV_PAD, :] = zeros
        pad_ref[base + CONV_PAD:base + CONV_PAD + row_len, :] = y[r * row_len:(r + 1) * row_len, :]
        pad_ref[base + CONV_PAD + row_len:base + stride, :] = zeros
    rows = []
    for r in range(n_rows):
        base = r * stride + CONV_PAD - CONV_K // 2
        acc = dw_ref[0:1, :] * pad_ref[base:base + row_len, :]
        for k in range(1, CONV_K):
            acc = acc + dw_ref[k:k + 1, :] * pad_ref[base + k:base + k + row_len, :]
        rows.append(acc)
    conv = (jnp.concatenate(rows, axis=0) if n_rows > 1 else rows[0]) + dwb_ref[...]
    mu = jnp.mean(conv, axis=-1, keepdims=True)
    cen = conv - mu
    var = jnp.mean(cen * cen, axis=-1, keepdims=True)
    yl = cen * lax.rsqrt(var + EPS) * lng_ref[...] + lnb_ref[...]
    ys = (yl * _sigmoid(yl)).astype(BF16)
    y_conv = jnp.dot(ys, cproj_ref[...], preferred_element_type=F32)

    gates = jnp.dot(n, win_ref[:, P_GATE:P_AB], preferred_element_type=F32)
    ma_ref[...] = (_sigmoid(gates[:, :D]) * y_conv).astype(BF16)
    gd_ref[...] = _sigmoid(gates[:, D:]).astype(BF16)
    qkv_ref[...] = jnp.dot(n, win_ref[:, P_QKV:P_Z], preferred_element_type=F32).astype(BF16)
    zz = jnp.dot(n, win_ref[:, P_Z:P_GATE], preferred_element_type=F32)
    z_ref[...] = (zz * _sigmoid(zz)).astype(BF16)
    ab_ref[...] = jnp.dot(n, win_ref[:, P_AB:P_END], preferred_element_type=F32)


def _mixin(h, mods, layer, row_fn, nw, win, dw, dwb, lng, lnb, cproj, tm, row_len):
    n_tok = h.shape[0]
    tok = lambda w: pl.BlockSpec((tm, w), lambda i: (i, 0))
    pad_rows = (tm // row_len) * (row_len + 2 * CONV_PAD)
    return pl.pallas_call(
        functools.partial(_mixin_kernel, row_len=row_len),
        grid=(n_tok // tm,),
        in_specs=[tok(D), _mod_spec(layer, 1, row_fn), _resident((1, D)), _resident((D, P_END)),
                  _resident((CONV_K, D)), _resident((1, D)), _resident((1, D)), _resident((1, D)),
                  _resident((D, D))],
        out_specs=[tok(D), tok(3 * D), tok(D), tok(D), tok(AB_PAD)],
        out_shape=[jax.ShapeDtypeStruct((n_tok, D), BF16),
                   jax.ShapeDtypeStruct((n_tok, 3 * D), BF16),
                   jax.ShapeDtypeStruct((n_tok, D), BF16),
                   jax.ShapeDtypeStruct((n_tok, D), BF16),
                   jax.ShapeDtypeStruct((n_tok, AB_PAD), F32)],
        scratch_shapes=[pltpu.VMEM((pad_rows, D), F32)],
        compiler_params=_cparams("parallel"),
        name="mixin",
    )(h, mods, nw, win, dw, dwb, lng, lnb, cproj)


def _dot_nt(a, b):
    return lax.dot_general(a, b, (((1,), (1,)), ((), ())), preferred_element_type=F32)


def _dot_tn(a, b):
    return lax.dot_general(a, b, (((0,), (0,)), ((), ())), preferred_element_type=F32)


def _dn_kernel(*refs, rev, add_prev):
    if add_prev:
        (qkv_ref, qprev_ref, qnext_ref, ab_ref, sw_ref, nega_ref, dtb_ref, tri_ref, s0_ref, oprev_ref,
         o_ref, sfin_ref, s_ref, q_s, k_s, v_s, gct_s) = refs
    else:
        (qkv_ref, qprev_ref, qnext_ref, ab_ref, sw_ref, nega_ref, dtb_ref, tri_ref, s0_ref,
         o_ref, sfin_ref, s_ref, q_s, k_s, v_s, gct_s) = refs
        oprev_ref = None
    bt = qkv_ref.shape[0]
    n_chunks = bt // CHUNK
    j = pl.program_id(1)
    nb = pl.num_programs(1)
    blk = (nb - 1 - j) if rev else j

    @pl.when(j == 0)
    def _():
        s_ref[...] = s0_ref[...]

    x = qkv_ref[...].astype(F32)
    prev_row = jnp.where(blk > 0, qprev_ref[HALO - 1:HALO, :].astype(F32), 0.0)
    next_row = jnp.where(blk < nb - 1, qnext_ref[0:1, :].astype(F32), 0.0)
    rid = lax.broadcasted_iota(jnp.int32, (bt, 1), 0)
    x_prev = jnp.where(rid == 0, prev_row, pltpu.roll(x, 1, 0))
    x_next = jnp.where(rid == bt - 1, next_row, pltpu.roll(x, bt - 1, 0))
    y = sw_ref[0:1, :] * x_prev + sw_ref[1:2, :] * x + sw_ref[2:3, :] * x_next
    y = y * _sigmoid(y)
    for hh in range(H):
        qh = y[:, hh * DK:(hh + 1) * DK]
        kh = y[:, D + hh * DK:D + (hh + 1) * DK]
        q_s[:, hh * DK:(hh + 1) * DK] = qh * (lax.rsqrt(jnp.sum(qh * qh, axis=-1, keepdims=True) + EPS)
                                              * (DK ** -0.5))
        k_s[:, hh * DK:(hh + 1) * DK] = kh * lax.rsqrt(jnp.sum(kh * kh, axis=-1, keepdims=True) + EPS)
    v_s[...] = y[:, 2 * D:]

    ab = ab_ref[...]
    sp_in = ab + dtb_ref[...]
    softplus = jnp.maximum(sp_in, 0.0) + jnp.log1p(jnp.exp(-jnp.abs(sp_in)))
    g_all = nega_ref[...] * softplus
    beta_all = _sigmoid(ab)
    gc = jnp.dot(tri_ref[...], g_all, preferred_element_type=F32, precision=lax.Precision.HIGHEST)
    gct_s[...] = gc.T

    d = 1 if rev else 0
    ri = lax.broadcasted_iota(jnp.int32, (CHUNK, CHUNK), 0)
    ci = lax.broadcasted_iota(jnp.int32, (CHUNK, CHUNK), 1)
    incl = (ci >= ri) if rev else (ci <= ri)
    strict = (ci > ri) if rev else (ci < ri)
    eye = jnp.where(ri == ci, 1.0, 0.0).astype(F32)

    for step in range(n_chunks):
        c = (n_chunks - 1 - step) if rev else step
        r0 = c * CHUNK
        last = r0 if rev else r0 + CHUNK - 1
        for hh in range(H):
            lane = d * H + hh
            hs = slice(hh * DK, (hh + 1) * DK)
            q = q_s[r0:r0 + CHUNK, hs]
            k = k_s[r0:r0 + CHUNK, hs]
            v = v_s[r0:r0 + CHUNK, hs]
            bcol = beta_all[r0:r0 + CHUNK, 2 * H + lane:2 * H + lane + 1]
            gcol = gc[r0:r0 + CHUNK, lane:lane + 1]
            grow = gct_s[lane:lane + 1, r0:r0 + CHUNK]
            glast = gc[last:last + 1, lane:lane + 1]
            diff = gcol - grow
            dec = jnp.where(incl, jnp.exp(jnp.where(incl, diff, 0.0)), 0.0)
            kb = k * bcol
            kf = k.astype(BF16)
            a_mat = jnp.where(strict, _dot_nt(kb.astype(BF16), kf) * dec, 0.0)
            qk = _dot_nt(q.astype(BF16), kf) * dec
            p = a_mat
            t_inv = eye - a_mat
            for _ in range(5):
                pb = p.astype(BF16)
                p = jnp.dot(pb, pb, preferred_element_type=F32)
                t_inv = t_inv + jnp.dot(t_inv.astype(BF16), p.astype(BF16), preferred_element_type=F32)
            eg = jnp.exp(gcol)
            rhs = jnp.concatenate([v * bcol, kb * eg], axis=1).astype(BF16)
            sol = jnp.dot(t_inv.astype(BF16), rhs, preferred_element_type=F32)
            u = sol[:, :DK]
            w = sol[:, DK:]
            s_old = s_ref[hh]
            sb = s_old.astype(BF16)
            wq = jnp.concatenate([w, q * eg], axis=0).astype(BF16)
            wqs = jnp.dot(wq, sb, preferred_element_type=F32)
            v_new = u - wqs[:CHUNK]
            vb = v_new.astype(BF16)
            o = wqs[CHUNK:] + jnp.dot(qk.astype(BF16), vb, preferred_element_type=F32)
            kt = (k * jnp.exp(glast - gcol)).astype(BF16)
            s_ref[hh] = s_old * jnp.exp(glast) + _dot_tn(kt, vb)
            if add_prev:
                o = o + oprev_ref[r0:r0 + CHUNK, hs]
            o_ref[r0:r0 + CHUNK, hs] = o

    @pl.when(j == nb - 1)
    def _():
        sfin_ref[...] = s_ref[...]


def _deltanet(qkv, ab, sw, nega, dtb, tri, s0, o_prev, rev):
    bsz, t, _ = qkv.shape
    bt = min(DN_BLOCK, t)
    nb = t // bt
    hb = bt // HALO
    n_halo = t // HALO
    blk = (lambda j: nb - 1 - j) if rev else (lambda j: j)
    main = lambda w: pl.BlockSpec((None, bt, w), lambda b, j: (b, blk(j), 0))
    in_specs = [main(3 * D),
                pl.BlockSpec((None, HALO, 3 * D), lambda b, j: (b, jnp.maximum(blk(j) * hb - 1, 0), 0)),
                pl.BlockSpec((None, HALO, 3 * D),
                             lambda b, j: (b, jnp.minimum((blk(j) + 1) * hb, n_halo - 1), 0)),
                main(AB_PAD),
                pl.BlockSpec((SHORT_K, 3 * D), lambda b, j: (0, 0)),
                pl.BlockSpec((1, AB_PAD), lambda b, j: (0, 0)),
                pl.BlockSpec((1, AB_PAD), lambda b, j: (0, 0)),
                pl.BlockSpec((bt, bt), lambda b, j: (0, 0)),
                pl.BlockSpec((None, H, DK, DK), lambda b, j: (b, 0, 0, 0))]
    args = [qkv, qkv, qkv, ab, sw, nega, dtb, tri, s0]
    if o_prev is not None:
        in_specs.append(main(D))
        args.append(o_prev)
    return pl.pallas_call(
        functools.partial(_dn_kernel, rev=rev, add_prev=o_prev is not None),
        grid=(bsz, nb),
        in_specs=in_specs,
        out_specs=[main(D), pl.BlockSpec((None, H, DK, DK), lambda b, j: (b, 0, 0, 0))],
        out_shape=[jax.ShapeDtypeStruct((bsz, t, D), F32),
                   jax.ShapeDtypeStruct((bsz, H, DK, DK), F32)],
        scratch_shapes=[pltpu.VMEM((H, DK, DK), F32),
                        pltpu.VMEM((bt, D), F32), pltpu.VMEM((bt, D), F32), pltpu.VMEM((bt, D), F32),
                        pltpu.VMEM((AB_PAD, bt), F32)],
        compiler_params=_cparams("parallel", "arbitrary"),
        name="deltanet_bwd" if rev else "deltanet_fwd",
    )(*args)


def _merge_kernel(o_ref, z_ref, gd_ref, ma_ref, h_ref, mod_ref, onorm_ref, dproj_ref, wout_ref, out_ref):
    o = o_ref[...]
    parts = []
    for hh in range(H):
        oh = o[:, hh * DK:(hh + 1) * DK]
        parts.append(oh * lax.rsqrt(jnp.mean(oh * oh, axis=-1, keepdims=True) + EPS))
    on = jnp.concatenate(parts, axis=1) * onorm_ref[...] * z_ref[...].astype(F32)
    y_dn = jnp.dot(on.astype(BF16), dproj_ref[...], preferred_element_type=F32)
    m = ma_ref[...].astype(F32) + gd_ref[...].astype(F32) * y_dn
    y = jnp.dot(m.astype(BF16), wout_ref[...], preferred_element_type=F32)
    out_ref[...] = h_ref[...] + mod_ref[...][:, 2 * D:] * y


def _merge(o, z, gd, ma, h, mods, layer, row_fn, onorm, dproj, wout, tm):
    n_tok = h.shape[0]
    tok = lambda: pl.BlockSpec((tm, D), lambda i: (i, 0))
    return pl.pallas_call(
        _merge_kernel,
        grid=(n_tok // tm,),
        in_specs=[tok(), tok(), tok(), tok(), tok(), _mod_spec(layer, 1, row_fn),
                  _resident((1, D)), _resident((D, D)), _resident((D, D))],
        out_specs=tok(),
        out_shape=jax.ShapeDtypeStruct((n_tok, D), F32),
        compiler_params=_cparams("parallel"),
        name="merge",
    )(o, z, gd, ma, h, mods, onorm, dproj, wout)


def _final_kernel(h_ref, g_ref, o_ref):
    h = h_ref[...]
    o_ref[...] = h * lax.rsqrt(jnp.mean(h * h, axis=-1, keepdims=True) + EPS) * g_ref[...]


def _final_norm(h, g, tm):
    n_tok = h.shape[0]
    return pl.pallas_call(
        _final_kernel,
        grid=(n_tok // tm,),
        in_specs=[pl.BlockSpec((tm, D), lambda i: (i, 0)), _resident((1, D))],
        out_specs=pl.BlockSpec((tm, D), lambda i: (i, 0)),
        out_shape=jax.ShapeDtypeStruct((n_tok, D), F32),
        compiler_params=_cparams("parallel"),
        name="final_norm",
    )(h, g)


def _permute_w_in(w_in):
    o_qkv, o_z, o_ab, o_gate, n_in = 2 * D, 5 * D, 6 * D, 6 * D + 4 * H, 6 * D + 4 * H + 2 * D
    pad = jnp.zeros(w_in.shape[:-1] + (AB_PAD - 4 * H,), w_in.dtype)
    return jnp.concatenate([w_in[..., :o_qkv], w_in[..., o_qkv:o_z], w_in[..., o_z:o_ab],
                            w_in[..., o_gate:n_in], w_in[..., o_ab:o_gate], pad], axis=-1)


def _scan_tri(bt, rev):
    r = jnp.arange(bt)[:, None]
    c = jnp.arange(bt)[None, :]
    same = (r // CHUNK) == (c // CHUNK)
    return (same & ((c >= r) if rev else (c <= r))).astype(F32)


def kernel(x, c, ctx, c_ctx, ada_w, ada_b, ffn1_norm, ffn1_w13, ffn1_w2, mix_norm, w_in, conv_dw, conv_dw_b, conv_ln_g, conv_ln_b, conv_proj, dn_short, dn_a_log, dn_dt_bias, dn_onorm, dn_proj, w_out, ffn2_norm, ffn2_w13, ffn2_w2, final_norm):
    bsz, t_lat, _ = x.shape
    t_ctx = ctx.shape[1]
    depth = ada_w.shape[0]
    assert bsz + 1 <= MOD_ROWS and t_lat % DN_BLOCK == 0 and t_ctx % CHUNK == 0 and t_ctx <= DN_BLOCK

    c_all = jnp.concatenate([c, c_ctx[None, :], jnp.zeros((MOD_ROWS - bsz - 1, D), F32)], axis=0)
    mods = _ada(c_all, ada_w, ada_b).reshape(depth, MOD_ROWS, 1, NMOD * D)

    tm = 256
    lat_tiles = t_lat // tm
    lat_row = lambda i: i // lat_tiles
    ctx_row = lambda i: bsz

    w13_1, w2_1 = ffn1_w13.astype(BF16), ffn1_w2.astype(BF16)
    w13_2, w2_2 = ffn2_w13.astype(BF16), ffn2_w2.astype(BF16)
    win = _permute_w_in(w_in).astype(BF16)
    cproj, dproj, wout = conv_proj.astype(BF16), dn_proj.astype(BF16), w_out.astype(BF16)
    row = lambda a: a.reshape(depth, 1, -1)
    n1, nm, n2 = row(ffn1_norm), row(mix_norm), row(ffn2_norm)
    dwb, lng, lnb = row(conv_dw_b), row(conv_ln_g), row(conv_ln_b)
    onorm = row(jnp.tile(dn_onorm, (1, H)))
    lane_pad = jnp.zeros((depth, 1, AB_PAD - 2 * H), F32)
    nega = jnp.concatenate([-jnp.exp(dn_a_log.astype(F32)).reshape(depth, 1, 2 * H), lane_pad], axis=-1)
    dtb = jnp.concatenate([dn_dt_bias.astype(F32).reshape(depth, 1, 2 * H), lane_pad], axis=-1)
    tris = {(t, rev): _scan_tri(min(DN_BLOCK, t), rev) for t in (t_ctx, t_lat) for rev in (False, True)}
    s_zero = jnp.zeros((bsz, H, DK, DK), F32)

    h = x.reshape(bsz * t_lat, D)
    hc = ctx.reshape(bsz * t_ctx, D)
    for l in range(depth):
        last = l == depth - 1
        h = _ffn(h, mods, l, 0, lat_row, n1[l], w13_1[l], w2_1[l], tm)
        hc = _ffn(hc, mods, l, 0, ctx_row, n1[l], w13_1[l], w2_1[l], tm)

        mix_args = (nm[l], win[l], conv_dw[l], dwb[l], lng[l], lnb[l], cproj[l])
        ma_l, qkv_l, z_l, gd_l, ab_l = _mixin(h, mods, l, lat_row, *mix_args, tm=tm, row_len=GRID_W)
        ma_c, qkv_c, z_c, gd_c, ab_c = _mixin(hc, mods, l, ctx_row, *mix_args, tm=tm, row_len=t_ctx)

        seq = lambda a, t: a.reshape(bsz, t, a.shape[-1])
        dn_args = (dn_short[l], nega[l], dtb[l])
        o_c, s_f = _deltanet(seq(qkv_c, t_ctx), seq(ab_c, t_ctx), *dn_args, tris[(t_ctx, False)],
                             s_zero, None, False)
        o_c, s_b = _deltanet(seq(qkv_c, t_ctx), seq(ab_c, t_ctx), *dn_args, tris[(t_ctx, True)],
                             s_zero, o_c, True)
        o_l, _ = _deltanet(seq(qkv_l, t_lat), seq(ab_l, t_lat), *dn_args, tris[(t_lat, False)],
                           s_f, None, False)
        o_l, _ = _deltanet(seq(qkv_l, t_lat), seq(ab_l, t_lat), *dn_args, tris[(t_lat, True)],
                           s_b, o_l, True)

        out_args = (onorm[l], dproj[l], wout[l])
        h = _merge(o_l.reshape(bsz * t_lat, D), z_l, gd_l, ma_l, h, mods, l, lat_row, *out_args, tm=tm)
        h = _ffn(h, mods, l, 2, lat_row, n2[l], w13_2[l], w2_2[l], tm)
        if not last:
            hc = _merge(o_c.reshape(bsz * t_ctx, D), z_c, gd_c, ma_c, hc, mods, l, ctx_row, *out_args, tm=tm)
            hc = _ffn(hc, mods, l, 2, ctx_row, n2[l], w13_2[l], w2_2[l], tm)
    out = _final_norm(h, final_norm.reshape(1, D), tm)
    return out.reshape(bsz, t_lat, D)
```

```python
import functools

import jax
import jax.numpy as jnp
from jax import lax
from jax.experimental import pallas as pl
from jax.experimental.pallas import tpu as pltpu

F32 = jnp.float32
BF16 = jnp.bfloat16

EPS = 1e-6
D = 1024
DFF = 2816
NMOD = 9
GRID_W = 64
CONV_K = 31
SUBLANES = 8
CONV_PAD = 16
SHORT_K = 3
H = 8
DK = 128
CHUNK = 64
DN_BLOCK = 256
HALO = 16
MOD_ROWS = 40

P_CONV = 0
P_QKV = P_CONV + 2 * D
P_Z = P_QKV + 3 * D
P_GATE = P_Z + D
P_AB = P_GATE + 2 * D
AB_PAD = 128
P_END = P_AB + AB_PAD

VMEM_LIMIT = 56 * 1024 * 1024


def _cparams(*sem):
    return pltpu.CompilerParams(dimension_semantics=sem, vmem_limit_bytes=VMEM_LIMIT)


def _resident(shape):
    nd = len(shape)
    return pl.BlockSpec(shape, lambda *_: (0,) * nd, pipeline_mode=pl.Buffered(1))


def _mod_spec(layer, part, row_fn):
    return pl.BlockSpec((None, None, 1, 3 * D), lambda i: (layer, row_fn(i), 0, part))


def _sigmoid(x):
    return jax.nn.sigmoid(x)


def _rms_mod(h, nw, shift, scale):
    ms = jnp.mean(h * h, axis=-1, keepdims=True)
    n = h * lax.rsqrt(ms + EPS) * nw
    return n * (1.0 + scale) + shift


def _ada_kernel(c_ref, w_ref, b_ref, o_ref):
    c = c_ref[...]
    sc = c * _sigmoid(c)
    o_ref[...] = jnp.dot(sc, w_ref[...], preferred_element_type=F32,
                         precision=lax.Precision.HIGHEST) + b_ref[...]


def _ada(c_all, ada_w, ada_b):
    depth = ada_w.shape[0]
    rows = c_all.shape[0]
    return pl.pallas_call(
        _ada_kernel,
        grid=(depth, NMOD),
        in_specs=[pl.BlockSpec((rows, D), lambda l, j: (0, 0)),
                  pl.BlockSpec((None, D, D), lambda l, j: (l, 0, j)),
                  pl.BlockSpec((None, 1, D), lambda l, j: (l, 0, j))],
        out_specs=pl.BlockSpec((None, rows, D), lambda l, j: (l, 0, j)),
        out_shape=jax.ShapeDtypeStruct((depth, rows, NMOD * D), F32),
        compiler_params=_cparams("parallel", "parallel"),
        name="ada",
    )(c_all, ada_w, ada_b.reshape(depth, 1, NMOD * D))


def _ffn_kernel(h_ref, mod_ref, nw_ref, w13_ref, w2_ref, o_ref):
    h = h_ref[...]
    mod = mod_ref[...]
    n = _rms_mod(h, nw_ref[...], mod[:, :D], mod[:, D:2 * D])
    ab = jnp.dot(n.astype(BF16), w13_ref[...], preferred_element_type=F32)
    a = ab[:, :DFF]
    b = ab[:, DFF:]
    s = (a * _sigmoid(a) * b).astype(BF16)
    y = jnp.dot(s, w2_ref[...], preferred_element_type=F32)
    o_ref[...] = h + (0.5 * mod[:, 2 * D:]) * y


def _ffn(h, mods, layer, part, row_fn, nw, w13, w2, tm):
    n_tok = h.shape[0]
    return pl.pallas_call(
        _ffn_kernel,
        grid=(n_tok // tm,),
        in_specs=[pl.BlockSpec((tm, D), lambda i: (i, 0)),
                  _mod_spec(layer, part, row_fn),
                  _resident((1, D)),
                  _resident((D, 2 * DFF)),
                  _resident((DFF, D))],
        out_specs=pl.BlockSpec((tm, D), lambda i: (i, 0)),
        out_shape=jax.ShapeDtypeStruct((n_tok, D), F32),
        compiler_params=_cparams("parallel"),
        name="ffn",
    )(h, mods, nw, w13, w2)


def _conv_span(row_len):
    return row_len + SUBLANES * ((CONV_K + SUBLANES - 1) // SUBLANES - 1)


def _mixin_kernel(h_ref, mod_ref, nw_ref, win_ref, dw_ref, dwb_ref, lng_ref, lnb_ref, cproj_ref,
                  ma_ref, qkv_ref, z_ref, gd_ref, ab_ref, pad_ref, rot_ref, *, row_len):
    tm = h_ref.shape[0]
    n_rows = tm // row_len
    stride = row_len + 2 * CONV_PAD
    mod = mod_ref[...]
    n = _rms_mod(h_ref[...], nw_ref[...], mod[:, :D], mod[:, D:2 * D]).astype(BF16)

    u = jnp.dot(n, win_ref[:, P_CONV:P_QKV], preferred_element_type=F32)
    y = u[:, :D] * _sigmoid(u[:, D:])
    zeros = jnp.zeros((CONV_PAD, D), F32)
    for r in range(n_rows):
        base = r * stride
        pad_ref[base:base + CONV_PAD, :] = zeros
        pad_ref[base + CONV_PAD:base + CONV_PAD + row_len, :] = y[r * row_len:(r + 1) * row_len, :]
        pad_ref[base + CONV_PAD + row_len:base + stride, :] = zeros

    gates = jnp.dot(n, win_ref[:, P_GATE:P_AB], preferred_element_type=F32)
    g_conv = _sigmoid(gates[:, :D])
    gd_ref[...] = _sigmoid(gates[:, D:]).astype(BF16)
    qkv_ref[...] = jnp.dot(n, win_ref[:, P_QKV:P_Z], preferred_element_type=F32).astype(BF16)
    zz = jnp.dot(n, win_ref[:, P_Z:P_GATE], preferred_element_type=F32)
    z_ref[...] = (zz * _sigmoid(zz)).astype(BF16)
    ab_ref[...] = jnp.dot(n, win_ref[:, P_AB:P_END], preferred_element_type=F32)

    span = _conv_span(row_len)
    first = CONV_PAD - CONV_K // 2
    rows = []
    for r in range(n_rows):
        for s in range(1, SUBLANES):
            rot_ref[s - 1] = pad_ref[r * stride + s:r * stride + s + span, :]
        acc = None
        for k in range(CONV_K):
            a, s = divmod(first + k, SUBLANES)
            if s == 0:
                tap = pad_ref[r * stride + SUBLANES * a:r * stride + SUBLANES * a + row_len, :]
            else:
                tap = rot_ref[s - 1, SUBLANES * a:SUBLANES * a + row_len, :]
            term = dw_ref[k:k + 1, :] * tap
            acc = term if acc is None else acc + term
        rows.append(acc)
    conv =(jnp.concatenate(rows, axis=0) if n_rows > 1 else rows[0]) + dwb_ref[...]
    mu = jnp.mean(conv, axis=-1, keepdims=True)
    cen = conv - mu
    var = jnp.mean(cen * cen, axis=-1, keepdims=True)
    yl = cen * lax.rsqrt(var + EPS) * lng_ref[...] + lnb_ref[...]
    ys = (yl * _sigmoid(yl)).astype(BF16)
    y_conv = jnp.dot(ys, cproj_ref[...], preferred_element_type=F32)
    ma_ref[...] = (g_conv * y_conv).astype(BF16)


def _mixin(h, mods, layer, row_fn, nw, win, dw, dwb, lng, lnb, cproj, tm, row_len):
    n_tok = h.shape[0]
    tok = lambda w: pl.BlockSpec((tm, w), lambda i: (i, 0))
    pad_rows = (tm // row_len) * (row_len + 2 * CONV_PAD)
    return pl.pallas_call(
        functools.partial(_mixin_kernel, row_len=row_len),
        grid=(n_tok // tm,),
        in_specs=[tok(D), _mod_spec(layer, 1, row_fn), _resident((1, D)), _resident((D, P_END)),
                  _resident((CONV_K, D)), _resident((1, D)), _resident((1, D)), _resident((1, D)),
                  _resident((D, D))],
        out_specs=[tok(D), tok(3 * D), tok(D), tok(D), tok(AB_PAD)],
        out_shape=[jax.ShapeDtypeStruct((n_tok, D), BF16),
                   jax.ShapeDtypeStruct((n_tok, 3 * D), BF16),
                   jax.ShapeDtypeStruct((n_tok, D), BF16),
                   jax.ShapeDtypeStruct((n_tok, D), BF16),
                   jax.ShapeDtypeStruct((n_tok, AB_PAD), F32)],
        scratch_shapes=[pltpu.VMEM((pad_rows, D), F32),
                        pltpu.VMEM((SUBLANES - 1, _conv_span(row_len), D), F32)],
        compiler_params=_cparams("parallel"),
        name="mixin",
    )(h, mods, nw, win, dw, dwb, lng, lnb, cproj)


def _dot_nt(a, b):
    return lax.dot_general(a, b, (((1,), (1,)), ((), ())), preferred_element_type=F32)


def _dot_tn(a, b):
    return lax.dot_general(a, b, (((0,), (0,)), ((), ())), preferred_element_type=F32)


def _dn_kernel(*refs, rev, add_prev):
    if add_prev:
        (qkv_ref, qprev_ref, qnext_ref, ab_ref, sw_ref, nega_ref, dtb_ref, tri_ref, s0_ref, oprev_ref,
         o_ref, sfin_ref, s_ref, q_s, k_s, v_s, gct_s) = refs
    else:
        (qkv_ref, qprev_ref, qnext_ref, ab_ref, sw_ref, nega_ref, dtb_ref, tri_ref, s0_ref,
         o_ref, sfin_ref, s_ref, q_s, k_s, v_s, gct_s) = refs
        oprev_ref = None
    bt = qkv_ref.shape[0]
    n_chunks = bt // CHUNK
    j = pl.program_id(1)
    nb = pl.num_programs(1)
    blk = (nb - 1 - j) if rev else j

    @pl.when(j == 0)
    def _():
        s_ref[...] = s0_ref[...]

    x = qkv_ref[...].astype(F32)
    prev_row = jnp.where(blk > 0, qprev_ref[HALO - 1:HALO, :].astype(F32), 0.0)
    next_row = jnp.where(blk < nb - 1, qnext_ref[0:1, :].astype(F32), 0.0)
    rid = lax.broadcasted_iota(jnp.int32, (bt, 1), 0)
    x_prev = jnp.where(rid == 0, prev_row, pltpu.roll(x, 1, 0))
    x_next = jnp.where(rid == bt - 1, next_row, pltpu.roll(x, bt - 1, 0))
    y = sw_ref[0:1, :] * x_prev + sw_ref[1:2, :] * x + sw_ref[2:3, :] * x_next
    y = y * _sigmoid(y)
    for hh in range(H):
        qh = y[:, hh * DK:(hh + 1) * DK]
        kh = y[:, D + hh * DK:D + (hh + 1) * DK]
        q_s[:, hh * DK:(hh + 1) * DK] = qh * (lax.rsqrt(jnp.sum(qh * qh, axis=-1, keepdims=True) + EPS)
                                              * (DK ** -0.5))
        k_s[:, hh * DK:(hh + 1) * DK] = kh * lax.rsqrt(jnp.sum(kh * kh, axis=-1, keepdims=True) + EPS)
    v_s[...] = y[:, 2 * D:]

    ab = ab_ref[...]
    sp_in = ab + dtb_ref[...]
    softplus = jnp.maximum(sp_in, 0.0) + jnp.log1p(jnp.exp(-jnp.abs(sp_in)))
    g_all = nega_ref[...] * softplus
    beta_all = _sigmoid(ab)
    gc = jnp.dot(tri_ref[...], g_all, preferred_element_type=F32, precision=lax.Precision.HIGHEST)
    gct_s[...] = gc.T

    d = 1 if rev else 0
    ri = lax.broadcasted_iota(jnp.int32, (CHUNK, CHUNK), 0)
    ci = lax.broadcasted_iota(jnp.int32, (CHUNK, CHUNK), 1)
    incl = (ci >= ri) if rev else (ci <= ri)
    strict = (ci > ri) if rev else (ci < ri)

    units = []
    for step in range(n_chunks):
        c = (n_chunks - 1 - step) if rev else step
        r0 = c * CHUNK
        last = r0 if rev else r0 + CHUNK - 1
        for hh in range(H):
            lane = d * H + hh
            hs = slice(hh * DK, (hh + 1) * DK)
            units.append(dict(step=step, hh=hh, rows=slice(r0, r0 + CHUNK), hs=hs,
                              bcol=beta_all[r0:r0 + CHUNK, 2 * H + lane:2 * H + lane + 1],
                              gcol=gc[r0:r0 + CHUNK, lane:lane + 1],
                              glast=gc[last:last + 1, lane:lane + 1],
                              lane=lane))

    for un in units:
        k = k_s[un["rows"], un["hs"]]
        q = q_s[un["rows"], un["hs"]]
        kb = k * un["bcol"]
        gram = _dot_nt(jnp.concatenate([kb, q], axis=0).astype(BF16), k.astype(BF16))
        grow = gct_s[un["lane"]:un["lane"] + 1, un["rows"]]
        diff = un["gcol"] - grow
        dec = jnp.where(incl, jnp.exp(jnp.where(incl, diff, 0.0)), 0.0)
        un["a"] = jnp.where(strict, gram[:CHUNK] * dec, 0.0)
        un["qk"] = (gram[CHUNK:] * dec).astype(BF16)
        eg = jnp.exp(un["gcol"])
        un["rhs"] = jnp.concatenate([v_s[un["rows"], un["hs"]] * un["bcol"], kb * eg], axis=1)
        un["qe"] = q * eg
        un["kt"] = (k * jnp.exp(un["glast"] - un["gcol"])).astype(BF16)
    eye = jnp.where(ri == ci, 1.0, 0.0).astype(F32)
    for un in units:
        un["p"] = un["a"].astype(BF16)
        un["t"] = eye - un["a"]
    for _ in range(5):
        for un in units:
            un["p"] = jnp.dot(un["p"], un["p"], preferred_element_type=F32).astype(BF16)
        for un in units:
            un["t"] = un["t"] + jnp.dot(un["t"].astype(BF16), un["p"], preferred_element_type=F32)
    for un in units:
        un["x"] = jnp.dot(un["t"].astype(BF16), un["rhs"].astype(BF16), preferred_element_type=F32)
        un["wq"] = jnp.concatenate([un["x"][:, DK:], un["qe"]], axis=0).astype(BF16)

    for step in range(n_chunks):
        cur = [un for un in units if un["step"] == step]
        for un in cur:
            un["s_old"] = s_ref[un["hh"]]
            un["wqs"] = jnp.dot(un["wq"], un["s_old"].astype(BF16), preferred_element_type=F32)
        for un in cur:
            vb = (un["x"][:, :DK] - un["wqs"][:CHUNK]).astype(BF16)
            o = un["wqs"][CHUNK:] + jnp.dot(un["qk"], vb, preferred_element_type=F32)
            s_ref[un["hh"]] = un["s_old"] * jnp.exp(un["glast"]) + _dot_tn(un["kt"], vb)
            if add_prev:
                o = o + oprev_ref[un["rows"], un["hs"]]
            o_ref[un["rows"], un["hs"]] = o

    @pl.when(j == nb - 1)
    def _():
        sfin_ref[...] = s_ref[...]


def _deltanet(qkv, ab, sw, nega, dtb, tri, s0, o_prev, rev):
    bsz, t, _ = qkv.shape
    bt = min(DN_BLOCK, t)
    nb = t // bt
    hb = bt // HALO
    n_halo = t // HALO
    blk = (lambda j: nb - 1 - j) if rev else (lambda j: j)
    main = lambda w: pl.BlockSpec((None, bt, w), lambda b, j: (b, blk(j), 0))
    in_specs = [main(3 * D),
                pl.BlockSpec((None, HALO, 3 * D), lambda b, j: (b, jnp.maximum(blk(j) * hb - 1, 0), 0)),
                pl.BlockSpec((None, HALO, 3 * D),
                             lambda b, j: (b, jnp.minimum((blk(j) + 1) * hb, n_halo - 1), 0)),
                main(AB_PAD),
                pl.BlockSpec((SHORT_K, 3 * D), lambda b, j: (0, 0)),
                pl.BlockSpec((1, AB_PAD), lambda b, j: (0, 0)),
                pl.BlockSpec((1, AB_PAD), lambda b, j: (0, 0)),
                pl.BlockSpec((bt, bt), lambda b, j: (0, 0)),
                pl.BlockSpec((None, H, DK, DK), lambda b, j: (b, 0, 0, 0))]
    args = [qkv, qkv, qkv, ab, sw, nega, dtb, tri, s0]
    if o_prev is not None:
        in_specs.append(main(D))
        args.append(o_prev)
    return pl.pallas_call(
        functools.partial(_dn_kernel, rev=rev, add_prev=o_prev is not None),
        grid=(bsz, nb),
        in_specs=in_specs,
        out_specs=[main(D), pl.BlockSpec((None, H, DK, DK), lambda b, j: (b, 0, 0, 0))],
        out_shape=[jax.ShapeDtypeStruct((bsz, t, D), F32),
                   jax.ShapeDtypeStruct((bsz, H, DK, DK), F32)],
        scratch_shapes=[pltpu.VMEM((H, DK, DK), F32),
                        pltpu.VMEM((bt, D), F32), pltpu.VMEM((bt, D), F32), pltpu.VMEM((bt, D), F32),
                        pltpu.VMEM((AB_PAD, bt), F32)],
        compiler_params=_cparams("parallel", "arbitrary"),
        name="deltanet_bwd" if rev else "deltanet_fwd",
    )(*args)


def _merge_kernel(o_ref, z_ref, gd_ref, ma_ref, h_ref, mod_ref, onorm_ref, dproj_ref, wout_ref, out_ref):
    o = o_ref[...]
    parts = []
    for hh in range(H):
        oh = o[:, hh * DK:(hh + 1) * DK]
        parts.append(oh * lax.rsqrt(jnp.mean(oh * oh, axis=-1, keepdims=True) + EPS))
    on = jnp.concatenate(parts, axis=1) * onorm_ref[...] * z_ref[...].astype(F32)
    y_dn = jnp.dot(on.astype(BF16), dproj_ref[...], preferred_element_type=F32)
    m = ma_ref[...].astype(F32) + gd_ref[...].astype(F32) * y_dn
    y = jnp.dot(m.astype(BF16), wout_ref[...], preferred_element_type=F32)
    out_ref[...] = h_ref[...] + mod_ref[...][:, 2 * D:] * y


def _merge(o, z, gd, ma, h, mods, layer, row_fn, onorm, dproj, wout, tm):
    n_tok = h.shape[0]
    tok = lambda: pl.BlockSpec((tm, D), lambda i: (i, 0))
    return pl.pallas_call(
        _merge_kernel,
        grid=(n_tok // tm,),
        in_specs=[tok(), tok(), tok(), tok(), tok(), _mod_spec(layer, 1, row_fn),
                  _resident((1, D)), _resident((D, D)), _resident((D, D))],
        out_specs=tok(),
        out_shape=jax.ShapeDtypeStruct((n_tok, D), F32),
        compiler_params=_cparams("parallel"),
        name="merge",
    )(o, z, gd, ma, h, mods, onorm, dproj, wout)


def _final_kernel(h_ref, g_ref, o_ref):
    h = h_ref[...]
    o_ref[...] = h * lax.rsqrt(jnp.mean(h * h, axis=-1, keepdims=True) + EPS) * g_ref[...]


def _final_norm(h, g, tm):
    n_tok = h.shape[0]
    return pl.pallas_call(
        _final_kernel,
        grid=(n_tok // tm,),
        in_specs=[pl.BlockSpec((tm, D), lambda i: (i, 0)), _resident((1, D))],
        out_specs=pl.BlockSpec((tm, D), lambda i: (i, 0)),
        out_shape=jax.ShapeDtypeStruct((n_tok, D), F32),
        compiler_params=_cparams("parallel"),
        name="final_norm",
    )(h, g)


def _permute_w_in(w_in):
    o_qkv, o_z, o_ab, o_gate, n_in = 2 * D, 5 * D, 6 * D, 6 * D + 4 * H, 6 * D + 4 * H + 2 * D
    pad = jnp.zeros(w_in.shape[:-1] + (AB_PAD - 4 * H,), w_in.dtype)
    return jnp.concatenate([w_in[..., :o_qkv], w_in[..., o_qkv:o_z], w_in[..., o_z:o_ab],
                            w_in[..., o_gate:n_in], w_in[..., o_ab:o_gate], pad], axis=-1)


def _scan_tri(bt, rev):
    r = jnp.arange(bt)[:, None]
    c = jnp.arange(bt)[None, :]
    same = (r // CHUNK) == (c // CHUNK)
    return (same & ((c >= r) if rev else (c <= r))).astype(F32)


def kernel(x, c, ctx, c_ctx, ada_w, ada_b, ffn1_norm, ffn1_w13, ffn1_w2, mix_norm, w_in, conv_dw, conv_dw_b, conv_ln_g, conv_ln_b, conv_proj, dn_short, dn_a_log, dn_dt_bias, dn_onorm, dn_proj, w_out, ffn2_norm, ffn2_w13, ffn2_w2, final_norm):
    bsz, t_lat, _ = x.shape
    t_ctx = ctx.shape[1]
    depth = ada_w.shape[0]
    assert bsz + 1 <= MOD_ROWS and t_lat % DN_BLOCK == 0 and t_ctx % CHUNK == 0 and t_ctx <= DN_BLOCK

    c_all = jnp.concatenate([c, c_ctx[None, :], jnp.zeros((MOD_ROWS - bsz - 1, D), F32)], axis=0)
    mods = _ada(c_all, ada_w, ada_b).reshape(depth, MOD_ROWS, 1, NMOD * D)

    tm = 256
    lat_tiles = t_lat // tm
    lat_row = lambda i: i // lat_tiles
    ctx_row = lambda i: bsz

    w13_1, w2_1 = ffn1_w13.astype(BF16), ffn1_w2.astype(BF16)
    w13_2, w2_2 = ffn2_w13.astype(BF16), ffn2_w2.astype(BF16)
    win = _permute_w_in(w_in).astype(BF16)
    cproj, dproj, wout = conv_proj.astype(BF16), dn_proj.astype(BF16), w_out.astype(BF16)
    row = lambda a: a.reshape(depth, 1, -1)
    n1, nm, n2 = row(ffn1_norm), row(mix_norm), row(ffn2_norm)
    dwb, lng, lnb = row(conv_dw_b), row(conv_ln_g), row(conv_ln_b)
    onorm = row(jnp.tile(dn_onorm, (1, H)))
    lane_pad = jnp.zeros((depth, 1, AB_PAD - 2 * H), F32)
    nega = jnp.concatenate([-jnp.exp(dn_a_log.astype(F32)).reshape(depth, 1, 2 * H), lane_pad], axis=-1)
    dtb = jnp.concatenate([dn_dt_bias.astype(F32).reshape(depth, 1, 2 * H), lane_pad], axis=-1)
    tris = {(t, rev): _scan_tri(min(DN_BLOCK, t), rev) for t in (t_ctx, t_lat) for rev in (False, True)}
    s_zero = jnp.zeros((bsz, H, DK, DK), F32)

    h = x.reshape(bsz * t_lat, D)
    hc = ctx.reshape(bsz * t_ctx, D)
    for l in range(depth):
        last = l == depth - 1
        h = _ffn(h, mods, l, 0, lat_row, n1[l], w13_1[l], w2_1[l], tm)
        hc = _ffn(hc, mods, l, 0, ctx_row, n1[l], w13_1[l], w2_1[l], tm)

        mix_args = (nm[l], win[l], conv_dw[l], dwb[l], lng[l], lnb[l], cproj[l])
        ma_l, qkv_l, z_l, gd_l, ab_l = _mixin(h, mods, l, lat_row, *mix_args, tm=tm, row_len=GRID_W)
        ma_c, qkv_c, z_c, gd_c, ab_c = _mixin(hc, mods, l, ctx_row, *mix_args, tm=tm, row_len=t_ctx)

        seq = lambda a, t: a.reshape(bsz, t, a.shape[-1])
        dn_args = (dn_short[l], nega[l], dtb[l])
        o_c, s_f = _deltanet(seq(qkv_c, t_ctx), seq(ab_c, t_ctx), *dn_args, tris[(t_ctx, False)],
                             s_zero, None, False)
        o_c, s_b = _deltanet(seq(qkv_c, t_ctx), seq(ab_c, t_ctx), *dn_args, tris[(t_ctx, True)],
                             s_zero, o_c, True)
        o_l, _ = _deltanet(seq(qkv_l, t_lat), seq(ab_l, t_lat), *dn_args, tris[(t_lat, False)],
                           s_f, None, False)
        o_l, _ = _deltanet(seq(qkv_l, t_lat), seq(ab_l, t_lat), *dn_args, tris[(t_lat, True)],
                           s_b, o_l, True)

        out_args = (onorm[l], dproj[l], wout[l])
        h = _merge(o_l.reshape(bsz * t_lat, D), z_l, gd_l, ma_l, h, mods, l, lat_row, *out_args, tm=tm)
        h = _ffn(h, mods, l, 2, lat_row, n2[l], w13_2[l], w2_2[l], tm)
        if not last:
            hc = _merge(o_c.reshape(bsz * t_ctx, D), z_c, gd_c, ma_c, hc, mods, l, ctx_row, *out_args, tm=tm)
            hc = _ffn(hc, mods, l, 2, ctx_row, n2[l], w13_2[l], w2_2[l], tm)
    out = _final_norm(h, final_norm.reshape(1, D), tm)
    return out.reshape(bsz, t_lat, D)
```

```python
import functools

import jax
import jax.numpy as jnp
from jax import lax
from jax.experimental import pallas as pl
from jax.experimental.pallas import tpu as pltpu

F32 = jnp.float32
BF16 = jnp.bfloat16

EPS = 1e-6
D = 1024
DFF = 2816
NMOD = 9
GRID_W = 64
CONV_K = 31
SUBLANES = 8
CONV_PAD = 16
SHORT_K = 3
H = 8
DK = 128
CHUNK = 64
DN_BLOCK = 256
HALO = 16
MOD_ROWS = 40

P_CONV = 0
P_QKV = P_CONV + 2 * D
P_Z = P_QKV + 3 * D
P_GATE = P_Z + D
P_AB = P_GATE + 2 * D
AB_PAD = 128
P_END = P_AB + AB_PAD

VMEM_LIMIT = 56 * 1024 * 1024


def _cparams(*sem):
    return pltpu.CompilerParams(dimension_semantics=sem, vmem_limit_bytes=VMEM_LIMIT)


def _resident(shape):
    nd = len(shape)
    return pl.BlockSpec(shape, lambda *_: (0,) * nd, pipeline_mode=pl.Buffered(1))


def _mod_spec(layer, part, row_fn):
    return pl.BlockSpec((None, None, 1, 3 * D), lambda i: (layer, row_fn(i), 0, part))


def _sigmoid(x):
    return jax.nn.sigmoid(x)


def _rms_mod(h, nw, shift, scale):
    ms = jnp.mean(h * h, axis=-1, keepdims=True)
    n = h * lax.rsqrt(ms + EPS) * nw
    return n * (1.0 + scale) + shift


def _ada_kernel(c_ref, w_ref, b_ref, o_ref):
    c = c_ref[...]
    sc = (c * _sigmoid(c)).astype(BF16)
    o_ref[...] = jnp.dot(sc, w_ref[...].astype(BF16), preferred_element_type=F32) + b_ref[...]


def _ada(c_all, ada_w, ada_b):
    depth = ada_w.shape[0]
    rows = c_all.shape[0]
    return pl.pallas_call(
        _ada_kernel,
        grid=(depth, NMOD),
        in_specs=[pl.BlockSpec((rows, D), lambda l, j: (0, 0)),
                  pl.BlockSpec((None, D, D), lambda l, j: (l, 0, j)),
                  pl.BlockSpec((None, 1, D), lambda l, j: (l, 0, j))],
        out_specs=pl.BlockSpec((None, rows, D), lambda l, j: (l, 0, j)),
        out_shape=jax.ShapeDtypeStruct((depth, rows, NMOD * D), F32),
        compiler_params=_cparams("parallel", "parallel"),
        name="ada",
    )(c_all, ada_w, ada_b.reshape(depth, 1, NMOD * D))


def _ffn_step(h, mod, nw, w13_ref, w2_ref):
    n = _rms_mod(h, nw, mod[:, :D], mod[:, D:2 * D])
    ab = jnp.dot(n.astype(BF16), w13_ref[...], preferred_element_type=F32)
    a = ab[:, :DFF]
    b = ab[:, DFF:]
    s = (a * _sigmoid(a) * b).astype(BF16)
    y = jnp.dot(s, w2_ref[...], preferred_element_type=F32)
    return h + (0.5 * mod[:, 2 * D:]) * y


def _ffn_kernel(h_ref, mod_ref, nw_ref, w13_ref, w2_ref, o_ref):
    o_ref[...] = _ffn_step(h_ref[...], mod_ref[...], nw_ref[...], w13_ref, w2_ref)


def _ffn(h, mods, layer, part, row_fn, nw, w13, w2, tm):
    n_tok = h.shape[0]
    return pl.pallas_call(
        _ffn_kernel,
        grid=(n_tok // tm,),
        in_specs=[pl.BlockSpec((tm, D), lambda i: (i, 0)),
                  _mod_spec(layer, part, row_fn),
                  _resident((1, D)),
                  _resident((D, 2 * DFF)),
                  _resident((DFF, D))],
        out_specs=pl.BlockSpec((tm, D), lambda i: (i, 0)),
        out_shape=jax.ShapeDtypeStruct((n_tok, D), F32),
        compiler_params=_cparams("parallel"),
        name="ffn",
    )(h, mods, nw, w13, w2)


def _conv_span(row_len):
    return row_len + SUBLANES * ((CONV_K + SUBLANES - 1) // SUBLANES - 1)


def _mixin_kernel(h_ref, mod_ref, nw_ref, win_ref, dw_ref, dwb_ref, lng_ref, lnb_ref, cproj_ref,
                  ma_ref, qkv_ref, z_ref, gd_ref, ab_ref, pad_ref, rot_ref, *, row_len):
    tm = h_ref.shape[0]
    n_rows = tm // row_len
    stride = row_len + 2 * CONV_PAD
    mod = mod_ref[...]
    n = _rms_mod(h_ref[...], nw_ref[...], mod[:, :D], mod[:, D:2 * D]).astype(BF16)

    u = jnp.dot(n, win_ref[:, P_CONV:P_QKV], preferred_element_type=F32)
    y = u[:, :D] * _sigmoid(u[:, D:])
    zeros = jnp.zeros((CONV_PAD, D), F32)
    for r in range(n_rows):
        base = r * stride
        pad_ref[base:base + CONV_PAD, :] = zeros
        pad_ref[base + CONV_PAD:base + CONV_PAD + row_len, :] = y[r * row_len:(r + 1) * row_len, :]
        pad_ref[base + CONV_PAD + row_len:base + stride, :] = zeros

    gates = jnp.dot(n, win_ref[:, P_GATE:P_AB], preferred_element_type=F32)
    g_conv = _sigmoid(gates[:, :D])
    gd_ref[...] = _sigmoid(gates[:, D:]).astype(BF16)
    qkv_ref[...] = jnp.dot(n, win_ref[:, P_QKV:P_Z], preferred_element_type=F32).astype(BF16)
    zz = jnp.dot(n, win_ref[:, P_Z:P_GATE], preferred_element_type=F32)
    z_ref[...] = (zz * _sigmoid(zz)).astype(BF16)
    ab_ref[...] = jnp.dot(n, win_ref[:, P_AB:P_END], preferred_element_type=F32)

    span = _conv_span(row_len)
    first = CONV_PAD - CONV_K // 2
    rows = []
    for r in range(n_rows):
        for s in range(1, SUBLANES):
            rot_ref[s - 1] = pad_ref[r * stride + s:r * stride + s + span, :]
        acc = None
        for k in range(CONV_K):
            a, s = divmod(first + k, SUBLANES)
            if s == 0:
                tap = pad_ref[r * stride + SUBLANES * a:r * stride + SUBLANES * a + row_len, :]
            else:
                tap = rot_ref[s - 1, SUBLANES * a:SUBLANES * a + row_len, :]
            term = dw_ref[k:k + 1, :] * tap
            acc = term if acc is None else acc + term
        rows.append(acc)
    conv =(jnp.concatenate(rows, axis=0) if n_rows > 1 else rows[0]) + dwb_ref[...]
    mu = jnp.mean(conv, axis=-1, keepdims=True)
    cen = conv - mu
    var = jnp.mean(cen * cen, axis=-1, keepdims=True)
    yl = cen * lax.rsqrt(var + EPS) * lng_ref[...] + lnb_ref[...]
    ys = (yl * _sigmoid(yl)).astype(BF16)
    y_conv = jnp.dot(ys, cproj_ref[...], preferred_element_type=F32)
    ma_ref[...] = (g_conv * y_conv).astype(BF16)


def _mixin(h, mods, layer, row_fn, nw, win, dw, dwb, lng, lnb, cproj, tm, row_len):
    n_tok = h.shape[0]
    tok = lambda w: pl.BlockSpec((tm, w), lambda i: (i, 0))
    pad_rows = (tm // row_len) * (row_len + 2 * CONV_PAD)
    return pl.pallas_call(
        functools.partial(_mixin_kernel, row_len=row_len),
        grid=(n_tok // tm,),
        in_specs=[tok(D), _mod_spec(layer, 1, row_fn), _resident((1, D)), _resident((D, P_END)),
                  _resident((CONV_K, D)), _resident((1, D)), _resident((1, D)), _resident((1, D)),
                  _resident((D, D))],
        out_specs=[tok(D), tok(3 * D), tok(D), tok(D), tok(AB_PAD)],
        out_shape=[jax.ShapeDtypeStruct((n_tok, D), BF16),
                   jax.ShapeDtypeStruct((n_tok, 3 * D), BF16),
                   jax.ShapeDtypeStruct((n_tok, D), BF16),
                   jax.ShapeDtypeStruct((n_tok, D), BF16),
                   jax.ShapeDtypeStruct((n_tok, AB_PAD), F32)],
        scratch_shapes=[pltpu.VMEM((pad_rows, D), F32),
                        pltpu.VMEM((SUBLANES - 1, _conv_span(row_len), D), F32)],
        compiler_params=_cparams("parallel"),
        name="mixin",
    )(h, mods, nw, win, dw, dwb, lng, lnb, cproj)


def _dot_nt(a, b):
    return lax.dot_general(a, b, (((1,), (1,)), ((), ())), preferred_element_type=F32)


def _dot_tn(a, b):
    return lax.dot_general(a, b, (((0,), (0,)), ((), ())), preferred_element_type=F32)


def _dn_kernel(*refs, rev, add_prev, prepped):
    refs = list(refs)
    if prepped:
        qkvp_ref = refs.pop(0)
    else:
        qkv_ref, qprev_ref, qnext_ref = refs[:3]
        del refs[:3]
    ab_ref = refs.pop(0)
    sw_ref = None if prepped else refs.pop(0)
    nega_ref, dtb_ref, tri_ref, s0_ref = refs[:4]
    del refs[:4]
    oprev_ref = refs.pop(0) if add_prev else None
    o_ref, sfin_ref = refs[:2]
    del refs[:2]
    qkvp_out = None if prepped else refs.pop(0)
    s_ref, q_s, k_s, v_s, gct_s = refs
    bt = ab_ref.shape[0]
    n_chunks = bt // CHUNK
    j = pl.program_id(1)
    nb = pl.num_programs(1)
    blk = (nb - 1 - j) if rev else j

    @pl.when(j == 0)
    def _():
        s_ref[...] = s0_ref[...]

    if prepped:
        q_s[...] = qkvp_ref[:, :D].astype(F32)
        k_s[...] = qkvp_ref[:, D:2 * D].astype(F32)
        v_s[...] = qkvp_ref[:, 2 * D:].astype(F32)
    else:
        x = qkv_ref[...].astype(F32)
        prev_row = jnp.where(blk > 0, qprev_ref[HALO - 1:HALO, :].astype(F32), 0.0)
        next_row = jnp.where(blk < nb - 1, qnext_ref[0:1, :].astype(F32), 0.0)
        rid = lax.broadcasted_iota(jnp.int32, (bt, 1), 0)
        x_prev = jnp.where(rid == 0, prev_row, pltpu.roll(x, 1, 0))
        x_next = jnp.where(rid == bt - 1, next_row, pltpu.roll(x, bt - 1, 0))
        y = sw_ref[0:1, :] * x_prev + sw_ref[1:2, :] * x + sw_ref[2:3, :] * x_next
        y = y * _sigmoid(y)
        for hh in range(H):
            qh = y[:, hh * DK:(hh + 1) * DK]
            kh = y[:, D + hh * DK:D + (hh + 1) * DK]
            q_s[:, hh * DK:(hh + 1) * DK] = qh * (lax.rsqrt(jnp.sum(qh * qh, axis=-1, keepdims=True) + EPS)
                                                  * (DK ** -0.5))
            k_s[:, hh * DK:(hh + 1) * DK] = kh * lax.rsqrt(jnp.sum(kh * kh, axis=-1, keepdims=True) + EPS)
        v_s[...] = y[:, 2 * D:]
        qkvp_out[:, :D] = q_s[...].astype(BF16)
        qkvp_out[:, D:2 * D] = k_s[...].astype(BF16)
        qkvp_out[:, 2 * D:] = y[:, 2 * D:].astype(BF16)

    ab = ab_ref[...]
    sp_in = ab + dtb_ref[...]
    softplus = jnp.maximum(sp_in, 0.0) + jnp.log1p(jnp.exp(-jnp.abs(sp_in)))
    g_all = nega_ref[...] * softplus
    beta_all = _sigmoid(ab)
    gc = jnp.dot(tri_ref[...], g_all, preferred_element_type=F32, precision=lax.Precision.HIGHEST)
    gct_s[...] = gc.T

    d = 1 if rev else 0
    ri = lax.broadcasted_iota(jnp.int32, (CHUNK, CHUNK), 0)
    ci = lax.broadcasted_iota(jnp.int32, (CHUNK, CHUNK), 1)
    incl = (ci >= ri) if rev else (ci <= ri)
    strict = (ci > ri) if rev else (ci < ri)

    units = []
    for step in range(n_chunks):
        c = (n_chunks - 1 - step) if rev else step
        r0 = c * CHUNK
        last = r0 if rev else r0 + CHUNK - 1
        for hh in range(H):
            lane = d * H + hh
            hs = slice(hh * DK, (hh + 1) * DK)
            units.append(dict(step=step, hh=hh, rows=slice(r0, r0 + CHUNK), hs=hs,
                              bcol=beta_all[r0:r0 + CHUNK, 2 * H + lane:2 * H + lane + 1],
                              gcol=gc[r0:r0 + CHUNK, lane:lane + 1],
                              glast=gc[last:last + 1, lane:lane + 1],
                              lane=lane))

    for un in units:
        k = k_s[un["rows"], un["hs"]]
        q = q_s[un["rows"], un["hs"]]
        kb = k * un["bcol"]
        gram = _dot_nt(jnp.concatenate([kb, q], axis=0).astype(BF16), k.astype(BF16))
        grow = gct_s[un["lane"]:un["lane"] + 1, un["rows"]]
        diff = un["gcol"] - grow
        dec = jnp.where(incl, jnp.exp(jnp.where(incl, diff, 0.0)), 0.0)
        un["a"] = jnp.where(strict, gram[:CHUNK] * dec, 0.0)
        un["qk"] = (gram[CHUNK:] * dec).astype(BF16)
        eg = jnp.exp(un["gcol"])
        un["rhs"] = jnp.concatenate([v_s[un["rows"], un["hs"]] * un["bcol"], kb * eg], axis=1)
        un["qe"] = q * eg
        un["kt"] = (k * jnp.exp(un["glast"] - un["gcol"])).astype(BF16)
    eye = jnp.where(ri == ci, 1.0, 0.0).astype(F32)
    rc = ri ^ ci
    for un in units:
        un["t"] = eye - jnp.where(rc < 2, un["a"], 0.0)
    size = 2
    while size < CHUNK:
        for un in units:
            off = jnp.where(rc >= size, jnp.where(rc < 2 * size, un["a"], 0.0), 0.0).astype(BF16)
            un["tb"] = un["t"].astype(BF16)
            un["y"] = jnp.dot(un["tb"], off, preferred_element_type=F32).astype(BF16)
        for un in units:
            un["t"] = un["t"] - jnp.dot(un["y"], un["tb"], preferred_element_type=F32)
        size *= 2
    for un in units:
        un["x"] = jnp.dot(un["t"].astype(BF16), un["rhs"].astype(BF16), preferred_element_type=F32)
        un["wq"] = jnp.concatenate([un["x"][:, DK:], un["qe"]], axis=0).astype(BF16)

    for step in range(n_chunks):
        cur = [un for un in units if un["step"] == step]
        for un in cur:
            un["s_old"] = s_ref[un["hh"]]
            un["wqs"] = jnp.dot(un["wq"], un["s_old"].astype(BF16), preferred_element_type=F32)
        for un in cur:
            vb = (un["x"][:, :DK] - un["wqs"][:CHUNK]).astype(BF16)
            o = un["wqs"][CHUNK:] + jnp.dot(un["qk"], vb, preferred_element_type=F32)
            s_ref[un["hh"]] = un["s_old"] * jnp.exp(un["glast"]) + _dot_tn(un["kt"], vb)
            if add_prev:
                o = o + oprev_ref[un["rows"], un["hs"]]
            o_ref[un["rows"], un["hs"]] = o

    @pl.when(j == nb - 1)
    def _():
        sfin_ref[...] = s_ref[...]


def _deltanet(qkv, ab, sw, nega, dtb, tri, s0, o_prev, rev, prepped):
    bsz, t, _ = qkv.shape
    bt = min(DN_BLOCK, t)
    nb = t // bt
    hb = bt // HALO
    n_halo = t // HALO
    blk = (lambda j: nb - 1 - j) if rev else (lambda j: j)
    main = lambda w: pl.BlockSpec((None, bt, w), lambda b, j: (b, blk(j), 0))
    const = lambda shape: pl.BlockSpec(shape, lambda b, j: (0,) * len(shape))
    state = pl.BlockSpec((None, H, DK, DK), lambda b, j: (b, 0, 0, 0))
    in_specs, args = [main(3 * D)], [qkv]
    if not prepped:
        in_specs += [pl.BlockSpec((None, HALO, 3 * D), lambda b, j: (b, jnp.maximum(blk(j) * hb - 1, 0), 0)),
                     pl.BlockSpec((None, HALO, 3 * D),
                                  lambda b, j: (b, jnp.minimum((blk(j) + 1) * hb, n_halo - 1), 0))]
        args += [qkv, qkv]
    in_specs.append(main(AB_PAD))
    args.append(ab)
    if not prepped:
        in_specs.append(const((SHORT_K, 3 * D)))
        args.append(sw)
    in_specs += [const((1, AB_PAD)), const((1, AB_PAD)), const((bt, bt)), state]
    args += [nega, dtb, tri, s0]
    if o_prev is not None:
        in_specs.append(main(D))
        args.append(o_prev)
    out_specs = [main(D), state]
    out_shape = [jax.ShapeDtypeStruct((bsz, t, D), F32), jax.ShapeDtypeStruct((bsz, H, DK, DK), F32)]
    if not prepped:
        out_specs.append(main(3 * D))
        out_shape.append(jax.ShapeDtypeStruct((bsz, t, 3 * D), BF16))
    res = pl.pallas_call(
        functools.partial(_dn_kernel, rev=rev, add_prev=o_prev is not None, prepped=prepped),
        grid=(bsz, nb),
        in_specs=in_specs,
        out_specs=out_specs,
        out_shape=out_shape,
        scratch_shapes=[pltpu.VMEM((H, DK, DK), F32),
                        pltpu.VMEM((bt, D), F32), pltpu.VMEM((bt, D), F32), pltpu.VMEM((bt, D), F32),
                        pltpu.VMEM((AB_PAD, bt), F32)],
        compiler_params=_cparams("parallel", "arbitrary"),
        name="deltanet_bwd" if rev else "deltanet_fwd",
    )(*args)
    return res if not prepped else (res[0], res[1], None)


def _merge_kernel(o_ref, z_ref, gd_ref, ma_ref, h_ref, mod_ref, mod2_ref, onorm_ref, dproj_ref, wout_ref,
                  nw2_ref, w13_ref, w2_ref, fin_ref, out_ref, *, final):
    o = o_ref[...]
    parts = []
    for hh in range(H):
        oh = o[:, hh * DK:(hh + 1) * DK]
        parts.append(oh * lax.rsqrt(jnp.mean(oh * oh, axis=-1, keepdims=True) + EPS))
    on = jnp.concatenate(parts, axis=1) * onorm_ref[...] * z_ref[...].astype(F32)
    y_dn = jnp.dot(on.astype(BF16), dproj_ref[...], preferred_element_type=F32)
    m = ma_ref[...].astype(F32) + gd_ref[...].astype(F32) * y_dn
    y = jnp.dot(m.astype(BF16), wout_ref[...], preferred_element_type=F32)
    h = h_ref[...] + mod_ref[...][:, 2 * D:] * y
    h = _ffn_step(h, mod2_ref[...], nw2_ref[...], w13_ref, w2_ref)
    if final:
        h = h * lax.rsqrt(jnp.mean(h * h, axis=-1, keepdims=True) + EPS) * fin_ref[...]
    out_ref[...] = h


def _merge(o, z, gd, ma, h, mods, layer, row_fn, onorm, dproj, wout, nw2, w13, w2, fin, final, tm):
    n_tok = h.shape[0]
    tok = lambda: pl.BlockSpec((tm, D), lambda i: (i, 0))
    return pl.pallas_call(
        functools.partial(_merge_kernel, final=final),
        grid=(n_tok // tm,),
        in_specs=[tok(), tok(), tok(), tok(), tok(), _mod_spec(layer, 1, row_fn), _mod_spec(layer, 2, row_fn),
                  _resident((1, D)), _resident((D, D)), _resident((D, D)),
                  _resident((1, D)), _resident((D, 2 * DFF)), _resident((DFF, D)), _resident((1, D))],
        out_specs=tok(),
        out_shape=jax.ShapeDtypeStruct((n_tok, D), F32),
        compiler_params=_cparams("parallel"),
        name="merge_ffn",
    )(o, z, gd, ma, h, mods, mods, onorm, dproj, wout, nw2, w13, w2, fin)


def _permute_w_in(w_in):
    o_qkv, o_z, o_ab, o_gate, n_in = 2 * D, 5 * D, 6 * D, 6 * D + 4 * H, 6 * D + 4 * H + 2 * D
    pad = jnp.zeros(w_in.shape[:-1] + (AB_PAD - 4 * H,), w_in.dtype)
    return jnp.concatenate([w_in[..., :o_qkv], w_in[..., o_qkv:o_z], w_in[..., o_z:o_ab],
                            w_in[..., o_gate:n_in], w_in[..., o_ab:o_gate], pad], axis=-1)


def _scan_tri(bt, rev):
    r = jnp.arange(bt)[:, None]
    c = jnp.arange(bt)[None, :]
    same = (r // CHUNK) == (c // CHUNK)
    return (same & ((c >= r) if rev else (c <= r))).astype(F32)


def kernel(x, c, ctx, c_ctx, ada_w, ada_b, ffn1_norm, ffn1_w13, ffn1_w2, mix_norm, w_in, conv_dw, conv_dw_b, conv_ln_g, conv_ln_b, conv_proj, dn_short, dn_a_log, dn_dt_bias, dn_onorm, dn_proj, w_out, ffn2_norm, ffn2_w13, ffn2_w2, final_norm):
    bsz, t_lat, _ = x.shape
    t_ctx = ctx.shape[1]
    depth = ada_w.shape[0]
    assert bsz + 1 <= MOD_ROWS and t_lat % DN_BLOCK == 0 and t_ctx % CHUNK == 0 and t_ctx <= DN_BLOCK

    c_all = jnp.concatenate([c, c_ctx[None, :], jnp.zeros((MOD_ROWS - bsz - 1, D), F32)], axis=0)
    mods = _ada(c_all, ada_w, ada_b).reshape(depth, MOD_ROWS, 1, NMOD * D)

    tm = 256
    lat_tiles = t_lat // tm
    lat_row = lambda i: i // lat_tiles
    ctx_row = lambda i: bsz

    w13_1, w2_1 = ffn1_w13.astype(BF16), ffn1_w2.astype(BF16)
    w13_2, w2_2 = ffn2_w13.astype(BF16), ffn2_w2.astype(BF16)
    win = _permute_w_in(w_in).astype(BF16)
    cproj, dproj, wout = conv_proj.astype(BF16), dn_proj.astype(BF16), w_out.astype(BF16)
    row = lambda a: a.reshape(depth, 1, -1)
    n1, nm, n2 = row(ffn1_norm), row(mix_norm), row(ffn2_norm)
    dwb, lng, lnb = row(conv_dw_b), row(conv_ln_g), row(conv_ln_b)
    onorm = row(jnp.tile(dn_onorm, (1, H)))
    lane_pad = jnp.zeros((depth, 1, AB_PAD - 2 * H), F32)
    nega = jnp.concatenate([-jnp.exp(dn_a_log.astype(F32)).reshape(depth, 1, 2 * H), lane_pad], axis=-1)
    dtb = jnp.concatenate([dn_dt_bias.astype(F32).reshape(depth, 1, 2 * H), lane_pad], axis=-1)
    tris = {(t, rev): _scan_tri(min(DN_BLOCK, t), rev) for t in (t_ctx, t_lat) for rev in (False, True)}
    s_zero = jnp.zeros((bsz, H, DK, DK), F32)

    h = x.reshape(bsz * t_lat, D)
    hc = ctx.reshape(bsz * t_ctx, D)
    for l in range(depth):
        last = l == depth - 1
        h = _ffn(h, mods, l, 0, lat_row, n1[l], w13_1[l], w2_1[l], tm)
        hc = _ffn(hc, mods, l, 0, ctx_row, n1[l], w13_1[l], w2_1[l], tm)

        mix_args = (nm[l], win[l], conv_dw[l], dwb[l], lng[l], lnb[l], cproj[l])
        ma_l, qkv_l, z_l, gd_l, ab_l = _mixin(h, mods, l, lat_row, *mix_args, tm=tm, row_len=GRID_W)
        ma_c, qkv_c, z_c, gd_c, ab_c = _mixin(hc, mods, l, ctx_row, *mix_args, tm=tm, row_len=t_ctx)

        seq = lambda a, t: a.reshape(bsz, t, a.shape[-1])
        dn_args = (dn_short[l], nega[l], dtb[l])
        o_c, s_f, qkvp_c = _deltanet(seq(qkv_c, t_ctx), seq(ab_c, t_ctx), *dn_args, tris[(t_ctx, False)],
                                     s_zero, None, False, False)
        o_c, s_b, _ = _deltanet(qkvp_c, seq(ab_c, t_ctx), *dn_args, tris[(t_ctx, True)],
                                s_zero, o_c, True, True)
        o_l, _, qkvp_l = _deltanet(seq(qkv_l, t_lat), seq(ab_l, t_lat), *dn_args, tris[(t_lat, False)],
                                   s_f, None, False, False)
        o_l, _, _ = _deltanet(qkvp_l, seq(ab_l, t_lat), *dn_args, tris[(t_lat, True)],
                              s_b, o_l, True, True)

        out_args = (onorm[l], dproj[l], wout[l], n2[l], w13_2[l], w2_2[l], final_norm.reshape(1, D))
        h = _merge(o_l.reshape(bsz * t_lat, D), z_l, gd_l, ma_l, h, mods, l, lat_row, *out_args,
                   final=last, tm=tm)
        if not last:
            hc = _merge(o_c.reshape(bsz * t_ctx, D), z_c, gd_c, ma_c, hc, mods, l, ctx_row, *out_args,
                        final=False, tm=tm)
    return h.reshape(bsz, t_lat, D)
```

```python
import functools

import jax
import jax.numpy as jnp
from jax import lax
from jax.experimental import pallas as pl
from jax.experimental.pallas import tpu as pltpu

F32 = jnp.float32
BF16 = jnp.bfloat16

EPS = 1e-6
D = 1024
DFF = 2816
NMOD = 9
GRID_W = 64
CONV_K = 31
SUBLANES = 8
CONV_PAD = 16
SHORT_K = 3
H = 8
DK = 128
CHUNK = 64
CHUNK_LOG2 = CHUNK.bit_length() - 1
DN_BLOCK = 256
HALO = 16
MOD_ROWS = 40

P_CONV = 0
P_QKV = P_CONV + 2 * D
P_Z = P_QKV + 3 * D
P_GATE = P_Z + D
P_AB = P_GATE + 2 * D
AB_PAD = 128
P_END = P_AB + AB_PAD

VMEM_LIMIT = 56 * 1024 * 1024


def _cparams(*sem):
    return pltpu.CompilerParams(dimension_semantics=sem, vmem_limit_bytes=VMEM_LIMIT)


def _resident(shape):
    nd = len(shape)
    return pl.BlockSpec(shape, lambda *_: (0,) * nd, pipeline_mode=pl.Buffered(1))


def _mod_spec(layer, part, row_fn):
    return pl.BlockSpec((None, None, 1, 3 * D), lambda i: (layer, row_fn(i), 0, part))


def _sigmoid(x):
    return jax.nn.sigmoid(x)


def _rms_mod(h, nw, shift, scale):
    ms = jnp.mean(h * h, axis=-1, keepdims=True)
    n = h * lax.rsqrt(ms + EPS) * nw
    return n * (1.0 + scale) + shift


def _ada_kernel(c_ref, w_ref, b_ref, o_ref):
    c = c_ref[...]
    sc = (c * _sigmoid(c)).astype(BF16)
    o_ref[...] = jnp.dot(sc, w_ref[...].astype(BF16), preferred_element_type=F32) + b_ref[...]


def _ada(c_all, ada_w, ada_b):
    depth = ada_w.shape[0]
    rows = c_all.shape[0]
    return pl.pallas_call(
        _ada_kernel,
        grid=(depth, NMOD),
        in_specs=[pl.BlockSpec((rows, D), lambda l, j: (0, 0)),
                  pl.BlockSpec((None, D, D), lambda l, j: (l, 0, j)),
                  pl.BlockSpec((None, 1, D), lambda l, j: (l, 0, j))],
        out_specs=pl.BlockSpec((None, rows, D), lambda l, j: (l, 0, j)),
        out_shape=jax.ShapeDtypeStruct((depth, rows, NMOD * D), F32),
        compiler_params=_cparams("parallel", "parallel"),
        name="ada",
    )(c_all, ada_w, ada_b.reshape(depth, 1, NMOD * D))


def _ffn_step(h, mod, nw, w13_ref, w2_ref):
    n = _rms_mod(h, nw, mod[:, :D], mod[:, D:2 * D])
    ab = jnp.dot(n.astype(BF16), w13_ref[...], preferred_element_type=F32)
    a = ab[:, :DFF]
    b = ab[:, DFF:]
    s = (a * _sigmoid(a) * b).astype(BF16)
    y = jnp.dot(s, w2_ref[...], preferred_element_type=F32)
    return h + (0.5 * mod[:, 2 * D:]) * y


def _ffn_kernel(h_ref, mod_ref, nw_ref, w13_ref, w2_ref, o_ref):
    o_ref[...] = _ffn_step(h_ref[...], mod_ref[...], nw_ref[...], w13_ref, w2_ref)


def _ffn(h, mods, layer, part, row_fn, nw, w13, w2, tm):
    n_tok = h.shape[0]
    return pl.pallas_call(
        _ffn_kernel,
        grid=(n_tok // tm,),
        in_specs=[pl.BlockSpec((tm, D), lambda i: (i, 0)),
                  _mod_spec(layer, part, row_fn),
                  _resident((1, D)),
                  _resident((D, 2 * DFF)),
                  _resident((DFF, D))],
        out_specs=pl.BlockSpec((tm, D), lambda i: (i, 0)),
        out_shape=jax.ShapeDtypeStruct((n_tok, D), F32),
        compiler_params=_cparams("parallel"),
        name="ffn",
    )(h, mods, nw, w13, w2)


def _conv_span(row_len):
    return row_len + SUBLANES * ((CONV_K + SUBLANES - 1) // SUBLANES - 1)


def _mixin_kernel(h_ref, mod_ref, nw_ref, win_ref, dw_ref, dwb_ref, lng_ref, lnb_ref, cproj_ref,
                  ma_ref, qkv_ref, z_ref, gd_ref, ab_ref, pad_ref, rot_ref, *, row_len):
    tm = h_ref.shape[0]
    n_rows = tm // row_len
    stride = row_len + 2 * CONV_PAD
    mod = mod_ref[...]
    n = _rms_mod(h_ref[...], nw_ref[...], mod[:, :D], mod[:, D:2 * D]).astype(BF16)

    u = jnp.dot(n, win_ref[:, P_CONV:P_QKV], preferred_element_type=F32)
    y = u[:, :D] * _sigmoid(u[:, D:])
    zeros = jnp.zeros((CONV_PAD, D), F32)
    for r in range(n_rows):
        base = r * stride
        pad_ref[base:base + CONV_PAD, :] = zeros
        pad_ref[base + CONV_PAD:base + CONV_PAD + row_len, :] = y[r * row_len:(r + 1) * row_len, :]
        pad_ref[base + CONV_PAD + row_len:base + stride, :] = zeros

    gates = jnp.dot(n, win_ref[:, P_GATE:P_AB], preferred_element_type=F32)
    g_conv = _sigmoid(gates[:, :D])
    gd_ref[...] = _sigmoid(gates[:, D:]).astype(BF16)
    qkv_ref[...] = jnp.dot(n, win_ref[:, P_QKV:P_Z], preferred_element_type=F32).astype(BF16)
    zz = jnp.dot(n, win_ref[:, P_Z:P_GATE], preferred_element_type=F32)
    z_ref[...] = (zz * _sigmoid(zz)).astype(BF16)
    ab_ref[...] = jnp.dot(n, win_ref[:, P_AB:P_END], preferred_element_type=F32)

    span = _conv_span(row_len)
    first = CONV_PAD - CONV_K // 2
    rows = []
    for r in range(n_rows):
        for s in range(1, SUBLANES):
            rot_ref[s - 1] = pad_ref[r * stride + s:r * stride + s + span, :]
        acc = None
        for k in range(CONV_K):
            a, s = divmod(first + k, SUBLANES)
            if s == 0:
                tap = pad_ref[r * stride + SUBLANES * a:r * stride + SUBLANES * a + row_len, :]
            else:
                tap = rot_ref[s - 1, SUBLANES * a:SUBLANES * a + row_len, :]
            term = dw_ref[k:k + 1, :] * tap
            acc = term if acc is None else acc + term
        rows.append(acc)
    conv =(jnp.concatenate(rows, axis=0) if n_rows > 1 else rows[0]) + dwb_ref[...]
    mu = jnp.mean(conv, axis=-1, keepdims=True)
    cen = conv - mu
    var = jnp.mean(cen * cen, axis=-1, keepdims=True)
    yl = cen * lax.rsqrt(var + EPS) * lng_ref[...] + lnb_ref[...]
    ys = (yl * _sigmoid(yl)).astype(BF16)
    y_conv = jnp.dot(ys, cproj_ref[...], preferred_element_type=F32)
    ma_ref[...] = (g_conv * y_conv).astype(BF16)


def _mixin(h, mods, layer, row_fn, nw, win, dw, dwb, lng, lnb, cproj, tm, row_len):
    n_tok = h.shape[0]
    tok = lambda w: pl.BlockSpec((tm, w), lambda i: (i, 0))
    pad_rows = (tm // row_len) * (row_len + 2 * CONV_PAD)
    return pl.pallas_call(
        functools.partial(_mixin_kernel, row_len=row_len),
        grid=(n_tok // tm,),
        in_specs=[tok(D), _mod_spec(layer, 1, row_fn), _resident((1, D)), _resident((D, P_END)),
                  _resident((CONV_K, D)), _resident((1, D)), _resident((1, D)), _resident((1, D)),
                  _resident((D, D))],
        out_specs=[tok(D), tok(3 * D), tok(D), tok(D), tok(AB_PAD)],
        out_shape=[jax.ShapeDtypeStruct((n_tok, D), BF16),
                   jax.ShapeDtypeStruct((n_tok, 3 * D), BF16),
                   jax.ShapeDtypeStruct((n_tok, D), BF16),
                   jax.ShapeDtypeStruct((n_tok, D), BF16),
                   jax.ShapeDtypeStruct((n_tok, AB_PAD), F32)],
        scratch_shapes=[pltpu.VMEM((pad_rows, D), F32),
                        pltpu.VMEM((SUBLANES - 1, _conv_span(row_len), D), F32)],
        compiler_params=_cparams("parallel"),
        name="mixin",
    )(h, mods, nw, win, dw, dwb, lng, lnb, cproj)


def _dot_nt(a, b):
    return lax.dot_general(a, b, (((1,), (1,)), ((), ())), preferred_element_type=F32)


def _dot_tn(a, b):
    return lax.dot_general(a, b, (((0,), (0,)), ((), ())), preferred_element_type=F32)


def _dn_kernel(*refs, rev, add_prev, prepped):
    refs = list(refs)
    if prepped:
        qkvp_ref = refs.pop(0)
    else:
        qkv_ref, qprev_ref, qnext_ref = refs[:3]
        del refs[:3]
    ab_ref = refs.pop(0)
    sw_ref = None if prepped else refs.pop(0)
    nega_ref, dtb_ref, tri_ref, s0_ref = refs[:4]
    del refs[:4]
    oprev_ref = refs.pop(0) if add_prev else None
    o_ref, sfin_ref = refs[:2]
    del refs[:2]
    qkvp_out = None if prepped else refs.pop(0)
    s_ref, q_s, k_s, v_s, gct_s = refs
    bt = ab_ref.shape[0]
    n_chunks = bt // CHUNK
    j = pl.program_id(1)
    nb = pl.num_programs(1)
    blk = (nb - 1 - j) if rev else j

    @pl.when(j == 0)
    def _():
        s_ref[...] = s0_ref[...]

    if prepped:
        q_s[...] = qkvp_ref[:, :D].astype(F32)
        k_s[...] = qkvp_ref[:, D:2 * D].astype(F32)
        v_s[...] = qkvp_ref[:, 2 * D:].astype(F32)
    else:
        x = qkv_ref[...].astype(F32)
        prev_row = jnp.where(blk > 0, qprev_ref[HALO - 1:HALO, :].astype(F32), 0.0)
        next_row = jnp.where(blk < nb - 1, qnext_ref[0:1, :].astype(F32), 0.0)
        rid = lax.broadcasted_iota(jnp.int32, (bt, 1), 0)
        x_prev = jnp.where(rid == 0, prev_row, pltpu.roll(x, 1, 0))
        x_next = jnp.where(rid == bt - 1, next_row, pltpu.roll(x, bt - 1, 0))
        y = sw_ref[0:1, :] * x_prev + sw_ref[1:2, :] * x + sw_ref[2:3, :] * x_next
        y = y * _sigmoid(y)
        for hh in range(H):
            qh = y[:, hh * DK:(hh + 1) * DK]
            kh = y[:, D + hh * DK:D + (hh + 1) * DK]
            q_s[:, hh * DK:(hh + 1) * DK] = qh * (lax.rsqrt(jnp.sum(qh * qh, axis=-1, keepdims=True) + EPS)
                                                  * (DK ** -0.5))
            k_s[:, hh * DK:(hh + 1) * DK] = kh * lax.rsqrt(jnp.sum(kh * kh, axis=-1, keepdims=True) + EPS)
        v_s[...] = y[:, 2 * D:]
        qkvp_out[:, :D] = q_s[...].astype(BF16)
        qkvp_out[:, D:2 * D] = k_s[...].astype(BF16)
        qkvp_out[:, 2 * D:] = y[:, 2 * D:].astype(BF16)

    ab = ab_ref[...]
    sp_in = ab + dtb_ref[...]
    softplus = jnp.maximum(sp_in, 0.0) + jnp.log1p(jnp.exp(-jnp.abs(sp_in)))
    g_all = nega_ref[...] * softplus
    beta_all = _sigmoid(ab)
    gc = jnp.dot(tri_ref[...], g_all, preferred_element_type=F32, precision=lax.Precision.HIGHEST)
    gct_s[...] = gc.T

    d = 1 if rev else 0
    ri = lax.broadcasted_iota(jnp.int32, (CHUNK, bt), 0)
    li = lax.broadcasted_iota(jnp.int32, (CHUNK, bt), 1)
    ci = li & (CHUNK - 1)
    in_chunk = [lax.shift_right_logical(li, CHUNK_LOG2) == c for c in range(n_chunks)]
    keep = [jnp.where(m, 1.0, 0.0).astype(BF16) for m in in_chunk]
    incl = (ci >= ri) if rev else (ci <= ri)
    strict = (ci > ri) if rev else (ci < ri)
    eye = jnp.where(ri == ci, 1.0, 0.0).astype(F32)
    rc = ri ^ ci

    def side_by_side(m):
        out = None
        for c in range(n_chunks):
            piece = jnp.where(in_chunk[c], m[c * CHUNK:(c + 1) * CHUNK, :], 0.0)
            out = piece if out is None else out + piece
        return out

    def block_diag(m):
        return jnp.concatenate([m * keep[c] for c in range(n_chunks)], axis=0)

    def per_chunk_rows(col, pick):
        return jnp.concatenate(
            [jnp.broadcast_to(col[c * CHUNK + pick:c * CHUNK + pick + 1, :], (CHUNK, 1)) for c in range(n_chunks)],
            axis=0)

    heads = []
    for hh in range(H):
        lane = d * H + hh
        hs = slice(hh * DK, (hh + 1) * DK)
        bcol = beta_all[:, 2 * H + lane:2 * H + lane + 1]
        gcol = gc[:, lane:lane + 1]
        k = k_s[:, hs]
        q = q_s[:, hs]
        kb = k * bcol
        gram = _dot_nt(jnp.concatenate([kb, q], axis=0).astype(BF16), k.astype(BF16))
        g_rows = gcol[0:CHUNK, :]
        for c in range(1, n_chunks):
            g_rows = jnp.where(in_chunk[c], gcol[c * CHUNK:(c + 1) * CHUNK, :], g_rows)
        diff = g_rows - gct_s[lane:lane + 1, :]
        dec = jnp.where(incl, jnp.exp(jnp.where(incl, diff, 0.0)), 0.0)
        a = jnp.where(strict, side_by_side(gram[:bt]) * dec, 0.0)
        eg = jnp.exp(gcol)
        glast = per_chunk_rows(gcol, 0 if rev else CHUNK - 1)
        heads.append(dict(
            hh=hh, hs=hs, a=a,
            qk=(side_by_side(gram[bt:]) * dec).astype(BF16),
            rhs=jnp.concatenate([v_s[:, hs] * bcol, kb * eg], axis=1).astype(BF16),
            qe=q * eg,
            kt=(k * jnp.exp(glast - gcol)).astype(BF16),
            decay=jnp.exp(glast),
            t=eye - jnp.where(rc < 2, a, 0.0)))
    size = 2
    while size < CHUNK:
        for hd in heads:
            off = jnp.where(rc >= size, jnp.where(rc < 2 * size, hd["a"], 0.0), 0.0).astype(BF16)
            hd["tb"] = hd["t"].astype(BF16)
            hd["y"] = jnp.dot(hd["tb"], block_diag(off), preferred_element_type=F32).astype(BF16)
        for hd in heads:
            hd["t"] = hd["t"] - jnp.dot(hd["y"], block_diag(hd["tb"]), preferred_element_type=F32)
        size *= 2
    for hd in heads:
        hd["x"] = jnp.dot(block_diag(hd["t"].astype(BF16)), hd["rhs"], preferred_element_type=F32)

    for step in range(n_chunks):
        c = (n_chunks - 1 - step) if rev else step
        rows = slice(c * CHUNK, (c + 1) * CHUNK)
        for hd in heads:
            hd["s_old"] = s_ref[hd["hh"]]
            wq = jnp.concatenate([hd["x"][rows, DK:], hd["qe"][rows, :]], axis=0).astype(BF16)
            hd["wqs"] = jnp.dot(wq, hd["s_old"].astype(BF16), preferred_element_type=F32)
        for hd in heads:
            vb = (hd["x"][rows, :DK] - hd["wqs"][:CHUNK]).astype(BF16)
            o = hd["wqs"][CHUNK:] + jnp.dot(hd["qk"][:, rows], vb, preferred_element_type=F32)
            s_ref[hd["hh"]] = (hd["s_old"] * hd["decay"][c * CHUNK:c * CHUNK + 1, :]
                               + _dot_tn(hd["kt"][rows, :], vb))
            if add_prev:
                o = o + oprev_ref[rows, hd["hs"]]
            o_ref[rows, hd["hs"]] = o

    @pl.when(j == nb - 1)
    def _():
        sfin_ref[...] = s_ref[...]


def _deltanet(qkv, ab, sw, nega, dtb, tri, s0, o_prev, rev, prepped):
    bsz, t, _ = qkv.shape
    bt = min(DN_BLOCK, t)
    nb = t // bt
    hb = bt // HALO
    n_halo = t // HALO
    blk = (lambda j: nb - 1 - j) if rev else (lambda j: j)
    main = lambda w: pl.BlockSpec((None, bt, w), lambda b, j: (b, blk(j), 0))
    const = lambda shape: pl.BlockSpec(shape, lambda b, j: (0,) * len(shape))
    state = pl.BlockSpec((None, H, DK, DK), lambda b, j: (b, 0, 0, 0))
    in_specs, args = [main(3 * D)], [qkv]
    if not prepped:
        in_specs += [pl.BlockSpec((None, HALO, 3 * D), lambda b, j: (b, jnp.maximum(blk(j) * hb - 1, 0), 0)),
                     pl.BlockSpec((None, HALO, 3 * D),
                                  lambda b, j: (b, jnp.minimum((blk(j) + 1) * hb, n_halo - 1), 0))]
        args += [qkv, qkv]
    in_specs.append(main(AB_PAD))
    args.append(ab)
    if not prepped:
        in_specs.append(const((SHORT_K, 3 * D)))
        args.append(sw)
    in_specs += [const((1, AB_PAD)), const((1, AB_PAD)), const((bt, bt)), state]
    args += [nega, dtb, tri, s0]
    if o_prev is not None:
        in_specs.append(main(D))
        args.append(o_prev)
    out_specs = [main(D), state]
    out_shape = [jax.ShapeDtypeStruct((bsz, t, D), F32), jax.ShapeDtypeStruct((bsz, H, DK, DK), F32)]
    if not prepped:
        out_specs.append(main(3 * D))
        out_shape.append(jax.ShapeDtypeStruct((bsz, t, 3 * D), BF16))
    res = pl.pallas_call(
        functools.partial(_dn_kernel, rev=rev, add_prev=o_prev is not None, prepped=prepped),
        grid=(bsz, nb),
        in_specs=in_specs,
        out_specs=out_specs,
        out_shape=out_shape,
        scratch_shapes=[pltpu.VMEM((H, DK, DK), F32),
                        pltpu.VMEM((bt, D), F32), pltpu.VMEM((bt, D), F32), pltpu.VMEM((bt, D), F32),
                        pltpu.VMEM((AB_PAD, bt), F32)],
        compiler_params=_cparams("parallel", "arbitrary"),
        name="deltanet_bwd" if rev else "deltanet_fwd",
    )(*args)
    return res if not prepped else (res[0], res[1], None)


def _merge_kernel(o_ref, z_ref, gd_ref, ma_ref, h_ref, mod_ref, mod2_ref, onorm_ref, dproj_ref, wout_ref,
                  nw2_ref, w13_ref, w2_ref, fin_ref, out_ref, *, final):
    o = o_ref[...]
    parts = []
    for hh in range(H):
        oh = o[:, hh * DK:(hh + 1) * DK]
        parts.append(oh * lax.rsqrt(jnp.mean(oh * oh, axis=-1, keepdims=True) + EPS))
    on = jnp.concatenate(parts, axis=1) * onorm_ref[...] * z_ref[...].astype(F32)
    y_dn = jnp.dot(on.astype(BF16), dproj_ref[...], preferred_element_type=F32)
    m = ma_ref[...].astype(F32) + gd_ref[...].astype(F32) * y_dn
    y = jnp.dot(m.astype(BF16), wout_ref[...], preferred_element_type=F32)
    h = h_ref[...] + mod_ref[...][:, 2 * D:] * y
    h = _ffn_step(h, mod2_ref[...], nw2_ref[...], w13_ref, w2_ref)
    if final:
        h = h * lax.rsqrt(jnp.mean(h * h, axis=-1, keepdims=True) + EPS) * fin_ref[...]
    out_ref[...] = h


def _merge(o, z, gd, ma, h, mods, layer, row_fn, onorm, dproj, wout, nw2, w13, w2, fin, final, tm):
    n_tok = h.shape[0]
    tok = lambda: pl.BlockSpec((tm, D), lambda i: (i, 0))
    return pl.pallas_call(
        functools.partial(_merge_kernel, final=final),
        grid=(n_tok // tm,),
        in_specs=[tok(), tok(), tok(), tok(), tok(), _mod_spec(layer, 1, row_fn), _mod_spec(layer, 2, row_fn),
                  _resident((1, D)), _resident((D, D)), _resident((D, D)),
                  _resident((1, D)), _resident((D, 2 * DFF)), _resident((DFF, D)), _resident((1, D))],
        out_specs=tok(),
        out_shape=jax.ShapeDtypeStruct((n_tok, D), F32),
        compiler_params=_cparams("parallel"),
        name="merge_ffn",
    )(o, z, gd, ma, h, mods, mods, onorm, dproj, wout, nw2, w13, w2, fin)


def _permute_w_in(w_in):
    o_qkv, o_z, o_ab, o_gate, n_in = 2 * D, 5 * D, 6 * D, 6 * D + 4 * H, 6 * D + 4 * H + 2 * D
    pad = jnp.zeros(w_in.shape[:-1] + (AB_PAD - 4 * H,), w_in.dtype)
    return jnp.concatenate([w_in[..., :o_qkv], w_in[..., o_qkv:o_z], w_in[..., o_z:o_ab],
                            w_in[..., o_gate:n_in], w_in[..., o_ab:o_gate], pad], axis=-1)


def _scan_tri(bt, rev):
    r = jnp.arange(bt)[:, None]
    c = jnp.arange(bt)[None, :]
    same = (r // CHUNK) == (c // CHUNK)
    return (same & ((c >= r) if rev else (c <= r))).astype(F32)


def kernel(x, c, ctx, c_ctx, ada_w, ada_b, ffn1_norm, ffn1_w13, ffn1_w2, mix_norm, w_in, conv_dw, conv_dw_b, conv_ln_g, conv_ln_b, conv_proj, dn_short, dn_a_log, dn_dt_bias, dn_onorm, dn_proj, w_out, ffn2_norm, ffn2_w13, ffn2_w2, final_norm):
    bsz, t_lat, _ = x.shape
    t_ctx = ctx.shape[1]
    depth = ada_w.shape[0]
    assert bsz + 1 <= MOD_ROWS and t_lat % DN_BLOCK == 0 and t_ctx % CHUNK == 0 and t_ctx <= DN_BLOCK

    c_all = jnp.concatenate([c, c_ctx[None, :], jnp.zeros((MOD_ROWS - bsz - 1, D), F32)], axis=0)
    mods = _ada(c_all, ada_w, ada_b).reshape(depth, MOD_ROWS, 1, NMOD * D)

    tm = 512
    lat_tiles = t_lat // tm
    lat_row = lambda i: i // lat_tiles
    ctx_row = lambda i: bsz

    w13_1, w2_1 = ffn1_w13.astype(BF16), ffn1_w2.astype(BF16)
    w13_2, w2_2 = ffn2_w13.astype(BF16), ffn2_w2.astype(BF16)
    win = _permute_w_in(w_in).astype(BF16)
    cproj, dproj, wout = conv_proj.astype(BF16), dn_proj.astype(BF16), w_out.astype(BF16)
    row = lambda a: a.reshape(depth, 1, -1)
    n1, nm, n2 = row(ffn1_norm), row(mix_norm), row(ffn2_norm)
    dwb, lng, lnb = row(conv_dw_b), row(conv_ln_g), row(conv_ln_b)
    onorm = row(jnp.tile(dn_onorm, (1, H)))
    lane_pad = jnp.zeros((depth, 1, AB_PAD - 2 * H), F32)
    nega = jnp.concatenate([-jnp.exp(dn_a_log.astype(F32)).reshape(depth, 1, 2 * H), lane_pad], axis=-1)
    dtb = jnp.concatenate([dn_dt_bias.astype(F32).reshape(depth, 1, 2 * H), lane_pad], axis=-1)
    tris = {(t, rev): _scan_tri(min(DN_BLOCK, t), rev) for t in (t_ctx, t_lat) for rev in (False, True)}
    s_zero = jnp.zeros((bsz, H, DK, DK), F32)

    h = x.reshape(bsz * t_lat, D)
    hc = ctx.reshape(bsz * t_ctx, D)
    for l in range(depth):
        last = l == depth - 1
        h = _ffn(h, mods, l, 0, lat_row, n1[l], w13_1[l], w2_1[l], tm)
        hc = _ffn(hc, mods, l, 0, ctx_row, n1[l], w13_1[l], w2_1[l], tm)

        mix_args = (nm[l], win[l], conv_dw[l], dwb[l], lng[l], lnb[l], cproj[l])
        ma_l, qkv_l, z_l, gd_l, ab_l = _mixin(h, mods, l, lat_row, *mix_args, tm=tm, row_len=GRID_W)
        ma_c, qkv_c, z_c, gd_c, ab_c = _mixin(hc, mods, l, ctx_row, *mix_args, tm=tm, row_len=t_ctx)

        seq = lambda a, t: a.reshape(bsz, t, a.shape[-1])
        dn_args = (dn_short[l], nega[l], dtb[l])
        o_c, s_f, qkvp_c = _deltanet(seq(qkv_c, t_ctx), seq(ab_c, t_ctx), *dn_args, tris[(t_ctx, False)],
                                     s_zero, None, False, False)
        o_c, s_b, _ = _deltanet(qkvp_c, seq(ab_c, t_ctx), *dn_args, tris[(t_ctx, True)],
                                s_zero, o_c, True, True)
        o_l, _, qkvp_l = _deltanet(seq(qkv_l, t_lat), seq(ab_l, t_lat), *dn_args, tris[(t_lat, False)],
                                   s_f, None, False, False)
        o_l, _, _ = _deltanet(qkvp_l, seq(ab_l, t_lat), *dn_args, tris[(t_lat, True)],
                              s_b, o_l, True, True)

        out_args = (onorm[l], dproj[l], wout[l], n2[l], w13_2[l], w2_2[l], final_norm.reshape(1, D))
        h = _merge(o_l.reshape(bsz * t_lat, D), z_l, gd_l, ma_l, h, mods, l, lat_row, *out_args,
                   final=last, tm=tm)
        if not last:
            hc = _merge(o_c.reshape(bsz * t_ctx, D), z_c, gd_c, ma_c, hc, mods, l, ctx_row, *out_args,
                        final=False, tm=tm)
    return h.reshape(bsz, t_lat, D)
```

```python
import functools

import jax
import jax.numpy as jnp
from jax import lax
from jax.experimental import pallas as pl
from jax.experimental.pallas import tpu as pltpu

F32 = jnp.float32
BF16 = jnp.bfloat16

EPS = 1e-6
D = 1024
DFF = 2816
NMOD = 9
GRID_W = 64
CONV_K = 31
SUBLANES = 8
CONV_PAD = 16
SHORT_K = 3
H = 8
DK = 128
CHUNK = 64
DN_BLOCK = 256
HALO = 16
MOD_ROWS = 40

P_CONV = 0
P_QKV = P_CONV + 2 * D
P_Z = P_QKV + 3 * D
P_GATE = P_Z + D
P_AB = P_GATE + 2 * D
AB_PAD = 128
P_END = P_AB + AB_PAD

VMEM_LIMIT = 56 * 1024 * 1024


def _cparams(*sem):
    return pltpu.CompilerParams(dimension_semantics=sem, vmem_limit_bytes=VMEM_LIMIT)


def _resident(shape):
    nd = len(shape)
    return pl.BlockSpec(shape, lambda *_: (0,) * nd, pipeline_mode=pl.Buffered(1))


def _mod_spec(layer, part, row_fn):
    return pl.BlockSpec((None, None, 1, 3 * D), lambda i: (layer, row_fn(i), 0, part))


def _sigmoid(x):
    return jax.nn.sigmoid(x)


def _rms_mod(h, nw, shift, scale):
    ms = jnp.mean(h * h, axis=-1, keepdims=True)
    n = h * lax.rsqrt(ms + EPS) * nw
    return n * (1.0 + scale) + shift


def _ada_kernel(c_ref, w_ref, b_ref, o_ref):
    c = c_ref[...]
    sc = (c * _sigmoid(c)).astype(BF16)
    o_ref[...] = jnp.dot(sc, w_ref[...].astype(BF16), preferred_element_type=F32) + b_ref[...]


def _ada(c_all, ada_w, ada_b):
    depth = ada_w.shape[0]
    rows = c_all.shape[0]
    return pl.pallas_call(
        _ada_kernel,
        grid=(depth, NMOD),
        in_specs=[pl.BlockSpec((rows, D), lambda l, j: (0, 0)),
                  pl.BlockSpec((None, D, D), lambda l, j: (l, 0, j)),
                  pl.BlockSpec((None, 1, D), lambda l, j: (l, 0, j))],
        out_specs=pl.BlockSpec((None, rows, D), lambda l, j: (l, 0, j)),
        out_shape=jax.ShapeDtypeStruct((depth, rows, NMOD * D), F32),
        compiler_params=_cparams("parallel", "parallel"),
        name="ada",
    )(c_all, ada_w, ada_b.reshape(depth, 1, NMOD * D))


def _ffn_step(h, mod, nw, w13_ref, w2_ref):
    n = _rms_mod(h, nw, mod[:, :D], mod[:, D:2 * D])
    ab = jnp.dot(n.astype(BF16), w13_ref[...], preferred_element_type=F32)
    a = ab[:, :DFF]
    b = ab[:, DFF:]
    s = (a * _sigmoid(a) * b).astype(BF16)
    y = jnp.dot(s, w2_ref[...], preferred_element_type=F32)
    return h + (0.5 * mod[:, 2 * D:]) * y


def _ffn_kernel(h_ref, mod_ref, nw_ref, w13_ref, w2_ref, o_ref):
    o_ref[...] = _ffn_step(h_ref[...], mod_ref[...], nw_ref[...], w13_ref, w2_ref)


def _ffn(h, mods, layer, part, row_fn, nw, w13, w2, tm):
    n_tok = h.shape[0]
    return pl.pallas_call(
        _ffn_kernel,
        grid=(n_tok // tm,),
        in_specs=[pl.BlockSpec((tm, D), lambda i: (i, 0)),
                  _mod_spec(layer, part, row_fn),
                  _resident((1, D)),
                  _resident((D, 2 * DFF)),
                  _resident((DFF, D))],
        out_specs=pl.BlockSpec((tm, D), lambda i: (i, 0)),
        out_shape=jax.ShapeDtypeStruct((n_tok, D), F32),
        compiler_params=_cparams("parallel"),
        name="ffn",
    )(h, mods, nw, w13, w2)


def _conv_span(row_len):
    return row_len + SUBLANES * ((CONV_K + SUBLANES - 1) // SUBLANES - 1)


def _mixin_kernel(h_ref, mod_ref, nw_ref, win_ref, dw_ref, dwb_ref, lng_ref, lnb_ref, cproj_ref,
                  ma_ref, qkv_ref, z_ref, gd_ref, ab_ref, pad_ref, rot_ref, *, row_len):
    tm = h_ref.shape[0]
    n_rows = tm // row_len
    stride = row_len + 2 * CONV_PAD
    mod = mod_ref[...]
    n = _rms_mod(h_ref[...], nw_ref[...], mod[:, :D], mod[:, D:2 * D]).astype(BF16)

    u = jnp.dot(n, win_ref[:, P_CONV:P_QKV], preferred_element_type=F32)
    y = u[:, :D] * _sigmoid(u[:, D:])
    zeros = jnp.zeros((CONV_PAD, D), F32)
    for r in range(n_rows):
        base = r * stride
        pad_ref[base:base + CONV_PAD, :] = zeros
        pad_ref[base + CONV_PAD:base + CONV_PAD + row_len, :] = y[r * row_len:(r + 1) * row_len, :]
        pad_ref[base + CONV_PAD + row_len:base + stride, :] = zeros

    gates = jnp.dot(n, win_ref[:, P_GATE:P_AB], preferred_element_type=F32)
    g_conv = _sigmoid(gates[:, :D])
    gd_ref[...] = _sigmoid(gates[:, D:]).astype(BF16)
    qkv_ref[...] = jnp.dot(n, win_ref[:, P_QKV:P_Z], preferred_element_type=F32).astype(BF16)
    zz = jnp.dot(n, win_ref[:, P_Z:P_GATE], preferred_element_type=F32)
    z_ref[...] = (zz * _sigmoid(zz)).astype(BF16)
    ab_ref[...] = jnp.dot(n, win_ref[:, P_AB:P_END], preferred_element_type=F32)

    span = _conv_span(row_len)
    first = CONV_PAD - CONV_K // 2
    rows = []
    for r in range(n_rows):
        for s in range(1, SUBLANES):
            rot_ref[s - 1] = pad_ref[r * stride + s:r * stride + s + span, :]
        acc = None
        for k in range(CONV_K):
            a, s = divmod(first + k, SUBLANES)
            if s == 0:
                tap = pad_ref[r * stride + SUBLANES * a:r * stride + SUBLANES * a + row_len, :]
            else:
                tap = rot_ref[s - 1, SUBLANES * a:SUBLANES * a + row_len, :]
            term = dw_ref[k:k + 1, :] * tap
            acc = term if acc is None else acc + term
        rows.append(acc)
    conv = (jnp.concatenate(rows, axis=0) if n_rows > 1 else rows[0]) + dwb_ref[...]
    mu = jnp.mean(conv, axis=-1, keepdims=True)
    cen = conv - mu
    var = jnp.mean(cen * cen, axis=-1, keepdims=True)
    yl = cen * lax.rsqrt(var + EPS) * lng_ref[...] + lnb_ref[...]
    ys = (yl * _sigmoid(yl)).astype(BF16)
    y_conv = jnp.dot(ys, cproj_ref[...], preferred_element_type=F32)
    ma_ref[...] = (g_conv * y_conv).astype(BF16)


def _mixin(h, mods, layer, row_fn, nw, win, dw, dwb, lng, lnb, cproj, tm, row_len):
    n_tok = h.shape[0]
    tok = lambda w: pl.BlockSpec((tm, w), lambda i: (i, 0))
    pad_rows = (tm // row_len) * (row_len + 2 * CONV_PAD)
    return pl.pallas_call(
        functools.partial(_mixin_kernel, row_len=row_len),
        grid=(n_tok // tm,),
        in_specs=[tok(D), _mod_spec(layer, 1, row_fn), _resident((1, D)), _resident((D, P_END)),
                  _resident((CONV_K, D)), _resident((1, D)), _resident((1, D)), _resident((1, D)),
                  _resident((D, D))],
        out_specs=[tok(D), tok(3 * D), tok(D), tok(D), tok(AB_PAD)],
        out_shape=[jax.ShapeDtypeStruct((n_tok, D), BF16),
                   jax.ShapeDtypeStruct((n_tok, 3 * D), BF16),
                   jax.ShapeDtypeStruct((n_tok, D), BF16),
                   jax.ShapeDtypeStruct((n_tok, D), BF16),
                   jax.ShapeDtypeStruct((n_tok, AB_PAD), F32)],
        scratch_shapes=[pltpu.VMEM((pad_rows, D), F32),
                        pltpu.VMEM((SUBLANES - 1, _conv_span(row_len), D), F32)],
        compiler_params=_cparams("parallel"),
        name="mixin",
    )(h, mods, nw, win, dw, dwb, lng, lnb, cproj)


def _dot_nt(a, b):
    return lax.dot_general(a, b, (((1,), (1,)), ((), ())), preferred_element_type=F32)


def _dot_tn(a, b):
    return lax.dot_general(a, b, (((0,), (0,)), ((), ())), preferred_element_type=F32)


def _dn_kernel(*refs, rev, add_prev, prepped):
    refs = list(refs)
    if prepped:
        qkvp_ref = refs.pop(0)
    else:
        qkv_ref, qprev_ref, qnext_ref = refs[:3]
        del refs[:3]
    ab_ref = refs.pop(0)
    sw_ref = None if prepped else refs.pop(0)
    nega_ref, dtb_ref, tri_ref, s0_ref = refs[:4]
    del refs[:4]
    oprev_ref = refs.pop(0) if add_prev else None
    o_ref, sfin_ref = refs[:2]
    del refs[:2]
    qkvp_out = None if prepped else refs.pop(0)
    s_ref, q_s, k_s, v_s, gct_s = refs
    bt = ab_ref.shape[0]
    n_chunks = bt // CHUNK
    j = pl.program_id(1)
    nb = pl.num_programs(1)
    blk = (nb - 1 - j) if rev else j

    @pl.when(j == 0)
    def _():
        s_ref[...] = s0_ref[...]

    if prepped:
        q_s[...] = qkvp_ref[:, :D].astype(F32)
        k_s[...] = qkvp_ref[:, D:2 * D].astype(F32)
        v_s[...] = qkvp_ref[:, 2 * D:].astype(F32)
    else:
        x = qkv_ref[...].astype(F32)
        prev_row = jnp.where(blk > 0, qprev_ref[HALO - 1:HALO, :].astype(F32), 0.0)
        next_row = jnp.where(blk < nb - 1, qnext_ref[0:1, :].astype(F32), 0.0)
        rid = lax.broadcasted_iota(jnp.int32, (bt, 1), 0)
        x_prev = jnp.where(rid == 0, prev_row, pltpu.roll(x, 1, 0))
        x_next = jnp.where(rid == bt - 1, next_row, pltpu.roll(x, bt - 1, 0))
        y = sw_ref[0:1, :] * x_prev + sw_ref[1:2, :] * x + sw_ref[2:3, :] * x_next
        y = y * _sigmoid(y)
        for hh in range(H):
            qh = y[:, hh * DK:(hh + 1) * DK]
            kh = y[:, D + hh * DK:D + (hh + 1) * DK]
            q_s[:, hh * DK:(hh + 1) * DK] = qh * (lax.rsqrt(jnp.sum(qh * qh, axis=-1, keepdims=True) + EPS)
                                                  * (DK ** -0.5))
            k_s[:, hh * DK:(hh + 1) * DK] = kh * lax.rsqrt(jnp.sum(kh * kh, axis=-1, keepdims=True) + EPS)
        v_s[...] = y[:, 2 * D:]
        qkvp_out[:, :D] = q_s[...].astype(BF16)
        qkvp_out[:, D:2 * D] = k_s[...].astype(BF16)
        qkvp_out[:, 2 * D:] = y[:, 2 * D:].astype(BF16)

    ab = ab_ref[...]
    sp_in = ab + dtb_ref[...]
    softplus = jnp.maximum(sp_in, 0.0) + jnp.log1p(jnp.exp(-jnp.abs(sp_in)))
    g_all = nega_ref[...] * softplus
    beta_all = _sigmoid(ab)
    gc = jnp.dot(tri_ref[...], g_all, preferred_element_type=F32, precision=lax.Precision.HIGHEST)
    gct_s[...] = gc.T

    d = 1 if rev else 0
    ri = lax.broadcasted_iota(jnp.int32, (CHUNK, CHUNK), 0)
    ci = lax.broadcasted_iota(jnp.int32, (CHUNK, CHUNK), 1)
    incl = (ci >= ri) if rev else (ci <= ri)
    strict = (ci > ri) if rev else (ci < ri)

    units = []
    for step in range(n_chunks):
        c = (n_chunks - 1 - step) if rev else step
        r0 = c * CHUNK
        last = r0 if rev else r0 + CHUNK - 1
        for hh in range(H):
            lane = d * H + hh
            hs = slice(hh * DK, (hh + 1) * DK)
            units.append(dict(step=step, hh=hh, rows=slice(r0, r0 + CHUNK), hs=hs,
                              bcol=beta_all[r0:r0 + CHUNK, 2 * H + lane:2 * H + lane + 1],
                              gcol=gc[r0:r0 + CHUNK, lane:lane + 1],
                              glast=gc[last:last + 1, lane:lane + 1],
                              lane=lane))

    for un in units:
        k = k_s[un["rows"], un["hs"]]
        q = q_s[un["rows"], un["hs"]]
        kb = k * un["bcol"]
        gram = _dot_nt(jnp.concatenate([kb, q], axis=0).astype(BF16), k.astype(BF16))
        grow = gct_s[un["lane"]:un["lane"] + 1, un["rows"]]
        diff = un["gcol"] - grow
        dec = jnp.where(incl, jnp.exp(jnp.where(incl, diff, 0.0)), 0.0)
        un["a"] = jnp.where(strict, gram[:CHUNK] * dec, 0.0)
        un["qk"] = (gram[CHUNK:] * dec).astype(BF16)
        eg = jnp.exp(un["gcol"])
        un["rhs"] = jnp.concatenate([v_s[un["rows"], un["hs"]] * un["bcol"], kb * eg], axis=1)
        un["qe"] = q * eg
        un["kt"] = (k * jnp.exp(un["glast"] - un["gcol"])).astype(BF16)
    eye = jnp.where(ri == ci, 1.0, 0.0).astype(F32)
    rc = ri ^ ci
    for un in units:
        un["t"] = eye - jnp.where(rc < 2, un["a"], 0.0)
    size = 2
    while size < CHUNK:
        for un in units:
            off = jnp.where(rc >= size, jnp.where(rc < 2 * size, un["a"], 0.0), 0.0).astype(BF16)
            un["tb"] = un["t"].astype(BF16)
            un["y"] = jnp.dot(un["tb"], off, preferred_element_type=F32).astype(BF16)
        for un in units:
            un["t"] = un["t"] - jnp.dot(un["y"], un["tb"], preferred_element_type=F32)
        size *= 2
    for un in units:
        un["x"] = jnp.dot(un["t"].astype(BF16), un["rhs"].astype(BF16), preferred_element_type=F32)
        un["wq"] = jnp.concatenate([un["x"][:, DK:], un["qe"]], axis=0).astype(BF16)

    for step in range(n_chunks):
        cur = [un for un in units if un["step"] == step]
        for un in cur:
            un["s_old"] = s_ref[un["hh"]]
            un["wqs"] = jnp.dot(un["wq"], un["s_old"].astype(BF16), preferred_element_type=F32)
        for un in cur:
            vb = (un["x"][:, :DK] - un["wqs"][:CHUNK]).astype(BF16)
            o = un["wqs"][CHUNK:] + jnp.dot(un["qk"], vb, preferred_element_type=F32)
            s_ref[un["hh"]] = un["s_old"] * jnp.exp(un["glast"]) + _dot_tn(un["kt"], vb)
            if add_prev:
                o = o + oprev_ref[un["rows"], un["hs"]]
            o_ref[un["rows"], un["hs"]] = o

    @pl.when(j == nb - 1)
    def _():
        sfin_ref[...] = s_ref[...]


def _deltanet(qkv, ab, sw, nega, dtb, tri, s0, o_prev, rev, prepped):
    bsz, t, _ = qkv.shape
    bt = min(DN_BLOCK, t)
    nb = t // bt
    hb = bt // HALO
    n_halo = t // HALO
    blk = (lambda j: nb - 1 - j) if rev else (lambda j: j)
    main = lambda w: pl.BlockSpec((None, bt, w), lambda b, j: (b, blk(j), 0))
    const = lambda shape: pl.BlockSpec(shape, lambda b, j: (0,) * len(shape))
    state = pl.BlockSpec((None, H, DK, DK), lambda b, j: (b, 0, 0, 0))
    in_specs, args = [main(3 * D)], [qkv]
    if not prepped:
        in_specs += [pl.BlockSpec((None, HALO, 3 * D), lambda b, j: (b, jnp.maximum(blk(j) * hb - 1, 0), 0)),
                     pl.BlockSpec((None, HALO, 3 * D),
                                  lambda b, j: (b, jnp.minimum((blk(j) + 1) * hb, n_halo - 1), 0))]
        args += [qkv, qkv]
    in_specs.append(main(AB_PAD))
    args.append(ab)
    if not prepped:
        in_specs.append(const((SHORT_K, 3 * D)))
        args.append(sw)
    in_specs += [const((1, AB_PAD)), const((1, AB_PAD)), const((bt, bt)), state]
    args += [nega, dtb, tri, s0]
    if o_prev is not None:
        in_specs.append(main(D))
        args.append(o_prev)
    out_specs = [main(D), state]
    out_shape = [jax.ShapeDtypeStruct((bsz, t, D), F32), jax.ShapeDtypeStruct((bsz, H, DK, DK), F32)]
    if not prepped:
        out_specs.append(main(3 * D))
        out_shape.append(jax.ShapeDtypeStruct((bsz, t, 3 * D), BF16))
    res = pl.pallas_call(
        functools.partial(_dn_kernel, rev=rev, add_prev=o_prev is not None, prepped=prepped),
        grid=(bsz, nb),
        in_specs=in_specs,
        out_specs=out_specs,
        out_shape=out_shape,
        scratch_shapes=[pltpu.VMEM((H, DK, DK), F32),
                        pltpu.VMEM((bt, D), F32), pltpu.VMEM((bt, D), F32), pltpu.VMEM((bt, D), F32),
                        pltpu.VMEM((AB_PAD, bt), F32)],
        compiler_params=_cparams("parallel", "arbitrary"),
        name="deltanet_bwd" if rev else "deltanet_fwd",
    )(*args)
    return res if not prepped else (res[0], res[1], None)


def _merge_kernel(o_ref, z_ref, gd_ref, ma_ref, h_ref, mod_ref, mod2_ref, onorm_ref, dproj_ref, wout_ref,
                  nw2_ref, w13_ref, w2_ref, fin_ref, out_ref, *, final):
    o = o_ref[...]
    parts = []
    for hh in range(H):
        oh = o[:, hh * DK:(hh + 1) * DK]
        parts.append(oh * lax.rsqrt(jnp.mean(oh * oh, axis=-1, keepdims=True) + EPS))
    on = jnp.concatenate(parts, axis=1) * onorm_ref[...] * z_ref[...].astype(F32)
    y_dn = jnp.dot(on.astype(BF16), dproj_ref[...], preferred_element_type=F32)
    m = ma_ref[...].astype(F32) + gd_ref[...].astype(F32) * y_dn
    y = jnp.dot(m.astype(BF16), wout_ref[...], preferred_element_type=F32)
    h = h_ref[...] + mod_ref[...][:, 2 * D:] * y
    h = _ffn_step(h, mod2_ref[...], nw2_ref[...], w13_ref, w2_ref)
    if final:
        h = h * lax.rsqrt(jnp.mean(h * h, axis=-1, keepdims=True) + EPS) * fin_ref[...]
    out_ref[...] = h


def _merge(o, z, gd, ma, h, mods, layer, row_fn, onorm, dproj, wout, nw2, w13, w2, fin, final, tm):
    n_tok = h.shape[0]
    tok = lambda: pl.BlockSpec((tm, D), lambda i: (i, 0))
    return pl.pallas_call(
        functools.partial(_merge_kernel, final=final),
        grid=(n_tok // tm,),
        in_specs=[tok(), tok(), tok(), tok(), tok(), _mod_spec(layer, 1, row_fn), _mod_spec(layer, 2, row_fn),
                  _resident((1, D)), _resident((D, D)), _resident((D, D)),
                  _resident((1, D)), _resident((D, 2 * DFF)), _resident((DFF, D)), _resident((1, D))],
        out_specs=tok(),
        out_shape=jax.ShapeDtypeStruct((n_tok, D), F32),
        compiler_params=_cparams("parallel"),
        name="merge_ffn",
    )(o, z, gd, ma, h, mods, mods, onorm, dproj, wout, nw2, w13, w2, fin)


def _permute_w_in(w_in):
    o_qkv, o_z, o_ab, o_gate, n_in = 2 * D, 5 * D, 6 * D, 6 * D + 4 * H, 6 * D + 4 * H + 2 * D
    pad = jnp.zeros(w_in.shape[:-1] + (AB_PAD - 4 * H,), w_in.dtype)
    return jnp.concatenate([w_in[..., :o_qkv], w_in[..., o_qkv:o_z], w_in[..., o_z:o_ab],
                            w_in[..., o_gate:n_in], w_in[..., o_ab:o_gate], pad], axis=-1)


def _scan_tri(bt, rev):
    r = jnp.arange(bt)[:, None]
    c = jnp.arange(bt)[None, :]
    same = (r // CHUNK) == (c // CHUNK)
    return (same & ((c >= r) if rev else (c <= r))).astype(F32)


def kernel(x, c, ctx, c_ctx, ada_w, ada_b, ffn1_norm, ffn1_w13, ffn1_w2, mix_norm, w_in, conv_dw, conv_dw_b, conv_ln_g, conv_ln_b, conv_proj, dn_short, dn_a_log, dn_dt_bias, dn_onorm, dn_proj, w_out, ffn2_norm, ffn2_w13, ffn2_w2, final_norm):
    bsz, t_lat, _ = x.shape
    t_ctx = ctx.shape[1]
    depth = ada_w.shape[0]
    assert bsz + 1 <= MOD_ROWS and t_lat % DN_BLOCK == 0 and t_ctx % CHUNK == 0 and t_ctx <= DN_BLOCK

    c_all = jnp.concatenate([c, c_ctx[None, :], jnp.zeros((MOD_ROWS - bsz - 1, D), F32)], axis=0)
    mods = _ada(c_all, ada_w, ada_b).reshape(depth, MOD_ROWS, 1, NMOD * D)

    tm = 512
    lat_tiles = t_lat // tm
    lat_row = lambda i: i // lat_tiles
    ctx_row = lambda i: bsz

    w13_1, w2_1 = ffn1_w13.astype(BF16), ffn1_w2.astype(BF16)
    w13_2, w2_2 = ffn2_w13.astype(BF16), ffn2_w2.astype(BF16)
    win = _permute_w_in(w_in).astype(BF16)
    cproj, dproj, wout = conv_proj.astype(BF16), dn_proj.astype(BF16), w_out.astype(BF16)
    row = lambda a: a.reshape(depth, 1, -1)
    n1, nm, n2 = row(ffn1_norm), row(mix_norm), row(ffn2_norm)
    dwb, lng, lnb = row(conv_dw_b), row(conv_ln_g), row(conv_ln_b)
    onorm = row(jnp.tile(dn_onorm, (1, H)))
    lane_pad = jnp.zeros((depth, 1, AB_PAD - 2 * H), F32)
    nega = jnp.concatenate([-jnp.exp(dn_a_log.astype(F32)).reshape(depth, 1, 2 * H), lane_pad], axis=-1)
    dtb = jnp.concatenate([dn_dt_bias.astype(F32).reshape(depth, 1, 2 * H), lane_pad], axis=-1)
    tris = {(t, rev): _scan_tri(min(DN_BLOCK, t), rev) for t in (t_ctx, t_lat) for rev in (False, True)}
    s_zero = jnp.zeros((bsz, H, DK, DK), F32)

    h = x.reshape(bsz * t_lat, D)
    hc = ctx.reshape(bsz * t_ctx, D)
    for l in range(depth):
        last = l == depth - 1
        h = _ffn(h, mods, l, 0, lat_row, n1[l], w13_1[l], w2_1[l], tm)
        hc = _ffn(hc, mods, l, 0, ctx_row, n1[l], w13_1[l], w2_1[l], tm)

        mix_args = (nm[l], win[l], conv_dw[l], dwb[l], lng[l], lnb[l], cproj[l])
        ma_l, qkv_l, z_l, gd_l, ab_l = _mixin(h, mods, l, lat_row, *mix_args, tm=tm, row_len=GRID_W)
        ma_c, qkv_c, z_c, gd_c, ab_c = _mixin(hc, mods, l, ctx_row, *mix_args, tm=tm, row_len=t_ctx)

        seq = lambda a, t: a.reshape(bsz, t, a.shape[-1])
        dn_args = (dn_short[l], nega[l], dtb[l])
        o_c, s_f, qkvp_c = _deltanet(seq(qkv_c, t_ctx), seq(ab_c, t_ctx), *dn_args, tris[(t_ctx, False)],
                                     s_zero, None, False, False)
        o_c, s_b, _ = _deltanet(qkvp_c, seq(ab_c, t_ctx), *dn_args, tris[(t_ctx, True)],
                                s_zero, o_c, True, True)
        o_l, _, qkvp_l = _deltanet(seq(qkv_l, t_lat), seq(ab_l, t_lat), *dn_args, tris[(t_lat, False)],
                                   s_f, None, False, False)
        o_l, _, _ = _deltanet(qkvp_l, seq(ab_l, t_lat), *dn_args, tris[(t_lat, True)],
                              s_b, o_l, True, True)

        out_args = (onorm[l], dproj[l], wout[l], n2[l], w13_2[l], w2_2[l], final_norm.reshape(1, D))
        h = _merge(o_l.reshape(bsz * t_lat, D), z_l, gd_l, ma_l, h, mods, l, lat_row, *out_args,
                   final=last, tm=tm)
        if not last:
            hc = _merge(o_c.reshape(bsz * t_ctx, D), z_c, gd_c, ma_c, hc, mods, l, ctx_row, *out_args,
                        final=False, tm=tm)
    return h.reshape(bsz, t_lat, D)
```

```python
import functools

import jax
import jax.numpy as jnp
from jax import lax
from jax.experimental import pallas as pl
from jax.experimental.pallas import tpu as pltpu

F32 = jnp.float32
BF16 = jnp.bfloat16

EPS = 1e-6
D = 1024
DFF = 2816
NMOD = 9
GRID_W = 64
CONV_K = 31
SUBLANES = 8
CONV_PAD = 16
SHORT_K = 3
H = 8
DK = 128
CHUNK = 64
DN_BLOCK = 256
HALO = 16
MOD_ROWS = 40

P_CONV = 0
P_QKV = P_CONV + 2 * D
P_Z = P_QKV + 3 * D
P_GATE = P_Z + D
P_AB = P_GATE + 2 * D
AB_PAD = 128
P_END = P_AB + AB_PAD

VMEM_LIMIT = 56 * 1024 * 1024


def _cparams(*sem):
    return pltpu.CompilerParams(dimension_semantics=sem, vmem_limit_bytes=VMEM_LIMIT)


def _resident(shape):
    nd = len(shape)
    return pl.BlockSpec(shape, lambda *_: (0,) * nd, pipeline_mode=pl.Buffered(1))


def _mod_spec(layer, part, row_fn):
    return pl.BlockSpec((None, None, 1, 3 * D), lambda i: (layer, row_fn(i), 0, part))


def _sigmoid(x):
    return jax.nn.sigmoid(x)


def _rms_mod(h, nw, shift, scale):
    ms = jnp.mean(h * h, axis=-1, keepdims=True)
    n = h * lax.rsqrt(ms + EPS) * nw
    return n * (1.0 + scale) + shift


def _ada_kernel(c_ref, w_ref, b_ref, o_ref):
    c = c_ref[...]
    sc = (c * _sigmoid(c)).astype(BF16)
    o_ref[...] = jnp.dot(sc, w_ref[...].astype(BF16), preferred_element_type=F32) + b_ref[...]


def _ada(c_all, ada_w, ada_b):
    depth = ada_w.shape[0]
    rows = c_all.shape[0]
    return pl.pallas_call(
        _ada_kernel,
        grid=(depth, NMOD),
        in_specs=[pl.BlockSpec((rows, D), lambda l, j: (0, 0)),
                  pl.BlockSpec((None, D, D), lambda l, j: (l, 0, j)),
                  pl.BlockSpec((None, 1, D), lambda l, j: (l, 0, j))],
        out_specs=pl.BlockSpec((None, rows, D), lambda l, j: (l, 0, j)),
        out_shape=jax.ShapeDtypeStruct((depth, rows, NMOD * D), F32),
        compiler_params=_cparams("parallel", "parallel"),
        name="ada",
    )(c_all, ada_w, ada_b.reshape(depth, 1, NMOD * D))


def _ffn_step(h, mod, nw, w13_ref, w2_ref):
    n = _rms_mod(h, nw, mod[:, :D], mod[:, D:2 * D])
    ab = jnp.dot(n.astype(BF16), w13_ref[...], preferred_element_type=F32)
    a = ab[:, :DFF]
    b = ab[:, DFF:]
    s = (a * _sigmoid(a) * b).astype(BF16)
    y = jnp.dot(s, w2_ref[...], preferred_element_type=F32)
    return h + (0.5 * mod[:, 2 * D:]) * y


def _ffn_kernel(h_ref, mod_ref, nw_ref, w13_ref, w2_ref, o_ref):
    o_ref[...] = _ffn_step(h_ref[...], mod_ref[...], nw_ref[...], w13_ref, w2_ref)


def _ffn(h, mods, layer, part, row_fn, nw, w13, w2, tm):
    n_tok = h.shape[0]
    return pl.pallas_call(
        _ffn_kernel,
        grid=(n_tok // tm,),
        in_specs=[pl.BlockSpec((tm, D), lambda i: (i, 0)),
                  _mod_spec(layer, part, row_fn),
                  _resident((1, D)),
                  _resident((D, 2 * DFF)),
                  _resident((DFF, D))],
        out_specs=pl.BlockSpec((tm, D), lambda i: (i, 0)),
        out_shape=jax.ShapeDtypeStruct((n_tok, D), F32),
        compiler_params=_cparams("parallel"),
        name="ffn",
    )(h, mods, nw, w13, w2)


def _conv_span(row_len):
    return row_len + SUBLANES * ((CONV_K + SUBLANES - 1) // SUBLANES - 1)


def _mixin_kernel(h_ref, mod_ref, nw_ref, win_ref, *rest, row_len, branches):
    tm = h_ref.shape[0]
    mod = mod_ref[...]
    n = _rms_mod(h_ref[...], nw_ref[...], mod[:, :D], mod[:, D:2 * D]).astype(BF16)
    if not branches:
        qkv_ref, ab_ref = rest
        qkv_ref[...] = jnp.dot(n, win_ref[:, P_QKV:P_Z], preferred_element_type=F32).astype(BF16)
        ab_ref[...] = jnp.dot(n, win_ref[:, P_AB:P_END], preferred_element_type=F32)
        return
    dw_ref, dwb_ref, lng_ref, lnb_ref, cproj_ref, ma_ref, qkv_ref, z_ref, gd_ref, ab_ref, pad_ref, rot_ref = rest
    n_rows = tm // row_len
    stride = row_len + 2 * CONV_PAD

    u = jnp.dot(n, win_ref[:, P_CONV:P_QKV], preferred_element_type=F32)
    y = u[:, :D] * _sigmoid(u[:, D:])
    zeros = jnp.zeros((CONV_PAD, D), F32)
    for r in range(n_rows):
        base = r * stride
        pad_ref[base:base + CONV_PAD, :] = zeros
        pad_ref[base + CONV_PAD:base + CONV_PAD + row_len, :] = y[r * row_len:(r + 1) * row_len, :]
        pad_ref[base + CONV_PAD + row_len:base + stride, :] = zeros

    gates = jnp.dot(n, win_ref[:, P_GATE:P_AB], preferred_element_type=F32)
    g_conv = _sigmoid(gates[:, :D])
    gd_ref[...] = _sigmoid(gates[:, D:]).astype(BF16)
    qkv_ref[...] = jnp.dot(n, win_ref[:, P_QKV:P_Z], preferred_element_type=F32).astype(BF16)
    zz = jnp.dot(n, win_ref[:, P_Z:P_GATE], preferred_element_type=F32)
    z_ref[...] = (zz * _sigmoid(zz)).astype(BF16)
    ab_ref[...] = jnp.dot(n, win_ref[:, P_AB:P_END], preferred_element_type=F32)

    span = _conv_span(row_len)
    first = CONV_PAD - CONV_K // 2
    rows = []
    for r in range(n_rows):
        for s in range(1, SUBLANES):
            rot_ref[s - 1] = pad_ref[r * stride + s:r * stride + s + span, :]
        acc = None
        for k in range(CONV_K):
            a, s = divmod(first + k, SUBLANES)
            if s == 0:
                tap = pad_ref[r * stride + SUBLANES * a:r * stride + SUBLANES * a + row_len, :]
            else:
                tap = rot_ref[s - 1, SUBLANES * a:SUBLANES * a + row_len, :]
            term = dw_ref[k:k + 1, :] * tap
            acc = term if acc is None else acc + term
        rows.append(acc)
    conv = (jnp.concatenate(rows, axis=0) if n_rows > 1 else rows[0]) + dwb_ref[...]
    mu = jnp.mean(conv, axis=-1, keepdims=True)
    cen = conv - mu
    var = jnp.mean(cen * cen, axis=-1, keepdims=True)
    yl = cen * lax.rsqrt(var + EPS) * lng_ref[...] + lnb_ref[...]
    ys = (yl * _sigmoid(yl)).astype(BF16)
    y_conv = jnp.dot(ys, cproj_ref[...], preferred_element_type=F32)
    ma_ref[...] = (g_conv * y_conv).astype(BF16)


def _mixin(h, mods, layer, row_fn, nw, win, dw, dwb, lng, lnb, cproj, tm, row_len, branches=True):
    n_tok = h.shape[0]
    tok = lambda w: pl.BlockSpec((tm, w), lambda i: (i, 0))
    if not branches:
        qkv, ab = pl.pallas_call(
            functools.partial(_mixin_kernel, row_len=row_len, branches=False),
            grid=(n_tok // tm,),
            in_specs=[tok(D), _mod_spec(layer, 1, row_fn), _resident((1, D)), _resident((D, P_END))],
            out_specs=[tok(3 * D), tok(AB_PAD)],
            out_shape=[jax.ShapeDtypeStruct((n_tok, 3 * D), BF16), jax.ShapeDtypeStruct((n_tok, AB_PAD), F32)],
            compiler_params=_cparams("parallel"),
            name="mixin_dn_only",
        )(h, mods, nw, win)
        return None, qkv, None, None, ab
    pad_rows = (tm // row_len) * (row_len + 2 * CONV_PAD)
    return pl.pallas_call(
        functools.partial(_mixin_kernel, row_len=row_len, branches=True),
        grid=(n_tok // tm,),
        in_specs=[tok(D), _mod_spec(layer, 1, row_fn), _resident((1, D)), _resident((D, P_END)),
                  _resident((CONV_K, D)), _resident((1, D)), _resident((1, D)), _resident((1, D)),
                  _resident((D, D))],
        out_specs=[tok(D), tok(3 * D), tok(D), tok(D), tok(AB_PAD)],
        out_shape=[jax.ShapeDtypeStruct((n_tok, D), BF16),
                   jax.ShapeDtypeStruct((n_tok, 3 * D), BF16),
                   jax.ShapeDtypeStruct((n_tok, D), BF16),
                   jax.ShapeDtypeStruct((n_tok, D), BF16),
                   jax.ShapeDtypeStruct((n_tok, AB_PAD), F32)],
        scratch_shapes=[pltpu.VMEM((pad_rows, D), F32),
                        pltpu.VMEM((SUBLANES - 1, _conv_span(row_len), D), F32)],
        compiler_params=_cparams("parallel"),
        name="mixin",
    )(h, mods, nw, win, dw, dwb, lng, lnb, cproj)


def _dot_nt(a, b):
    return lax.dot_general(a, b, (((1,), (1,)), ((), ())), preferred_element_type=F32)


def _dot_tn(a, b):
    return lax.dot_general(a, b, (((0,), (0,)), ((), ())), preferred_element_type=F32)


def _dn_kernel(*refs, rev, add_prev, prepped):
    refs = list(refs)
    if prepped:
        qkvp_ref = refs.pop(0)
    else:
        qkv_ref, qprev_ref, qnext_ref = refs[:3]
        del refs[:3]
    ab_ref = refs.pop(0)
    sw_ref = None if prepped else refs.pop(0)
    nega_ref, dtb_ref, tri_ref, s0_ref = refs[:4]
    del refs[:4]
    oprev_ref = refs.pop(0) if add_prev else None
    o_ref, sfin_ref = refs[:2]
    del refs[:2]
    qkvp_out = None if prepped else refs.pop(0)
    s_ref, q_s, k_s, v_s, gct_s = refs
    bt = ab_ref.shape[0]
    n_chunks = bt // CHUNK
    j = pl.program_id(1)
    nb = pl.num_programs(1)
    blk = (nb - 1 - j) if rev else j

    @pl.when(j == 0)
    def _():
        s_ref[...] = s0_ref[...]

    if prepped:
        q_s[...] = qkvp_ref[:, :D].astype(F32)
        k_s[...] = qkvp_ref[:, D:2 * D].astype(F32)
        v_s[...] = qkvp_ref[:, 2 * D:].astype(F32)
    else:
        x = qkv_ref[...].astype(F32)
        prev_row = jnp.where(blk > 0, qprev_ref[HALO - 1:HALO, :].astype(F32), 0.0)
        next_row = jnp.where(blk < nb - 1, qnext_ref[0:1, :].astype(F32), 0.0)
        rid = lax.broadcasted_iota(jnp.int32, (bt, 1), 0)
        x_prev = jnp.where(rid == 0, prev_row, pltpu.roll(x, 1, 0))
        x_next = jnp.where(rid == bt - 1, next_row, pltpu.roll(x, bt - 1, 0))
        y = sw_ref[0:1, :] * x_prev + sw_ref[1:2, :] * x + sw_ref[2:3, :] * x_next
        y = y * _sigmoid(y)
        for hh in range(H):
            qh = y[:, hh * DK:(hh + 1) * DK]
            kh = y[:, D + hh * DK:D + (hh + 1) * DK]
            q_s[:, hh * DK:(hh + 1) * DK] = qh * (lax.rsqrt(jnp.sum(qh * qh, axis=-1, keepdims=True) + EPS)
                                                  * (DK ** -0.5))
            k_s[:, hh * DK:(hh + 1) * DK] = kh * lax.rsqrt(jnp.sum(kh * kh, axis=-1, keepdims=True) + EPS)
        v_s[...] = y[:, 2 * D:]
        qkvp_out[:, :D] = q_s[...].astype(BF16)
        qkvp_out[:, D:2 * D] = k_s[...].astype(BF16)
        qkvp_out[:, 2 * D:] = y[:, 2 * D:].astype(BF16)

    ab = ab_ref[...]
    sp_in = ab + dtb_ref[...]
    softplus = jnp.maximum(sp_in, 0.0) + jnp.log1p(jnp.exp(-jnp.abs(sp_in)))
    g_all = nega_ref[...] * softplus
    beta_all = _sigmoid(ab)
    gc = jnp.dot(tri_ref[...], g_all, preferred_element_type=F32, precision=lax.Precision.HIGHEST)
    gct_s[...] = gc.T

    d = 1 if rev else 0
    ri = lax.broadcasted_iota(jnp.int32, (CHUNK, CHUNK), 0)
    ci = lax.broadcasted_iota(jnp.int32, (CHUNK, CHUNK), 1)
    incl = (ci >= ri) if rev else (ci <= ri)
    strict = (ci > ri) if rev else (ci < ri)

    units = []
    for step in range(n_chunks):
        c = (n_chunks - 1 - step) if rev else step
        r0 = c * CHUNK
        last = r0 if rev else r0 + CHUNK - 1
        for hh in range(H):
            lane = d * H + hh
            hs = slice(hh * DK, (hh + 1) * DK)
            units.append(dict(step=step, hh=hh, rows=slice(r0, r0 + CHUNK), hs=hs,
                              bcol=beta_all[r0:r0 + CHUNK, 2 * H + lane:2 * H + lane + 1],
                              gcol=gc[r0:r0 + CHUNK, lane:lane + 1],
                              glast=gc[last:last + 1, lane:lane + 1],
                              lane=lane))

    for un in units:
        k = k_s[un["rows"], un["hs"]]
        q = q_s[un["rows"], un["hs"]]
        kb = k * un["bcol"]
        gram = _dot_nt(jnp.concatenate([kb, q], axis=0).astype(BF16), k.astype(BF16))
        grow = gct_s[un["lane"]:un["lane"] + 1, un["rows"]]
        diff = un["gcol"] - grow
        dec = jnp.where(incl, jnp.exp(jnp.where(incl, diff, 0.0)), 0.0)
        un["a"] = jnp.where(strict, gram[:CHUNK] * dec, 0.0)
        un["qk"] = (gram[CHUNK:] * dec).astype(BF16)
        eg = jnp.exp(un["gcol"])
        un["rhs"] = jnp.concatenate([v_s[un["rows"], un["hs"]] * un["bcol"], kb * eg], axis=1)
        un["qe"] = q * eg
        un["kt"] = (k * jnp.exp(un["glast"] - un["gcol"])).T.astype(BF16)
    eye = jnp.where(ri == ci, 1.0, 0.0).astype(F32)
    rc = ri ^ ci
    for un in units:
        un["t"] = eye - jnp.where(rc < 2, un["a"], 0.0)
    size = 2
    while size < CHUNK:
        for un in units:
            off = jnp.where(rc >= size, jnp.where(rc < 2 * size, un["a"], 0.0), 0.0).astype(BF16)
            un["tb"] = un["t"].astype(BF16)
            un["y"] = jnp.dot(un["tb"], off, preferred_element_type=F32).astype(BF16)
        for un in units:
            un["t"] = un["t"] - jnp.dot(un["y"], un["tb"], preferred_element_type=F32)
        size *= 2
    for un in units:
        un["x"] = jnp.dot(un["t"].astype(BF16), un["rhs"].astype(BF16), preferred_element_type=F32)
        un["wq"] = jnp.concatenate([un["x"][:, DK:], un["qe"]], axis=0).astype(BF16)

    for step in range(n_chunks):
        cur = [un for un in units if un["step"] == step]
        for un in cur:
            un["s_old"] = s_ref[un["hh"]]
            un["wqs"] = jnp.dot(un["wq"], un["s_old"].astype(BF16), preferred_element_type=F32)
        for un in cur:
            vb = (un["x"][:, :DK] - un["wqs"][:CHUNK]).astype(BF16)
            o = un["wqs"][CHUNK:] + jnp.dot(un["qk"], vb, preferred_element_type=F32)
            s_ref[un["hh"]] = (un["s_old"] * jnp.exp(un["glast"])
                               + jnp.dot(un["kt"], vb, preferred_element_type=F32))
            if add_prev:
                o = o + oprev_ref[un["rows"], un["hs"]]
            o_ref[un["rows"], un["hs"]] = o

    @pl.when(j == nb - 1)
    def _():
        sfin_ref[...] = s_ref[...]


def _deltanet(qkv, ab, sw, nega, dtb, tri, s0, o_prev, rev, prepped):
    bsz, t, _ = qkv.shape
    bt = min(DN_BLOCK, t)
    nb = t // bt
    hb = bt // HALO
    n_halo = t // HALO
    blk = (lambda j: nb - 1 - j) if rev else (lambda j: j)
    main = lambda w: pl.BlockSpec((None, bt, w), lambda b, j: (b, blk(j), 0))
    const = lambda shape: pl.BlockSpec(shape, lambda b, j: (0,) * len(shape))
    state = pl.BlockSpec((None, H, DK, DK), lambda b, j: (b, 0, 0, 0))
    in_specs, args = [main(3 * D)], [qkv]
    if not prepped:
        in_specs += [pl.BlockSpec((None, HALO, 3 * D), lambda b, j: (b, jnp.maximum(blk(j) * hb - 1, 0), 0)),
                     pl.BlockSpec((None, HALO, 3 * D),
                                  lambda b, j: (b, jnp.minimum((blk(j) + 1) * hb, n_halo - 1), 0))]
        args += [qkv, qkv]
    in_specs.append(main(AB_PAD))
    args.append(ab)
    if not prepped:
        in_specs.append(const((SHORT_K, 3 * D)))
        args.append(sw)
    in_specs += [const((1, AB_PAD)), const((1, AB_PAD)), const((bt, bt)), state]
    args += [nega, dtb, tri, s0]
    if o_prev is not None:
        in_specs.append(main(D))
        args.append(o_prev)
    out_specs = [main(D), state]
    out_shape = [jax.ShapeDtypeStruct((bsz, t, D), F32), jax.ShapeDtypeStruct((bsz, H, DK, DK), F32)]
    if not prepped:
        out_specs.append(main(3 * D))
        out_shape.append(jax.ShapeDtypeStruct((bsz, t, 3 * D), BF16))
    res = pl.pallas_call(
        functools.partial(_dn_kernel, rev=rev, add_prev=o_prev is not None, prepped=prepped),
        grid=(bsz, nb),
        in_specs=in_specs,
        out_specs=out_specs,
        out_shape=out_shape,
        scratch_shapes=[pltpu.VMEM((H, DK, DK), F32),
                        pltpu.VMEM((bt, D), F32), pltpu.VMEM((bt, D), F32), pltpu.VMEM((bt, D), F32),
                        pltpu.VMEM((AB_PAD, bt), F32)],
        compiler_params=_cparams("parallel", "arbitrary"),
        name="deltanet_bwd" if rev else "deltanet_fwd",
    )(*args)
    return res if not prepped else (res[0], res[1], None)


def _merge_kernel(o_ref, z_ref, gd_ref, ma_ref, h_ref, mod_ref, mod2_ref, onorm_ref, dproj_ref, wout_ref,
                  nw2_ref, w13_ref, w2_ref, fin_ref, out_ref, *, final):
    o = o_ref[...]
    parts = []
    for hh in range(H):
        oh = o[:, hh * DK:(hh + 1) * DK]
        parts.append(oh * lax.rsqrt(jnp.mean(oh * oh, axis=-1, keepdims=True) + EPS))
    on = jnp.concatenate(parts, axis=1) * onorm_ref[...] * z_ref[...].astype(F32)
    y_dn = jnp.dot(on.astype(BF16), dproj_ref[...], preferred_element_type=F32)
    m = ma_ref[...].astype(F32) + gd_ref[...].astype(F32) * y_dn
    y = jnp.dot(m.astype(BF16), wout_ref[...], preferred_element_type=F32)
    h = h_ref[...] + mod_ref[...][:, 2 * D:] * y
    h = _ffn_step(h, mod2_ref[...], nw2_ref[...], w13_ref, w2_ref)
    if final:
        h = h * lax.rsqrt(jnp.mean(h * h, axis=-1, keepdims=True) + EPS) * fin_ref[...]
    out_ref[...] = h


def _merge(o, z, gd, ma, h, mods, layer, row_fn, onorm, dproj, wout, nw2, w13, w2, fin, final, tm):
    n_tok = h.shape[0]
    tok = lambda: pl.BlockSpec((tm, D), lambda i: (i, 0))
    return pl.pallas_call(
        functools.partial(_merge_kernel, final=final),
        grid=(n_tok // tm,),
        in_specs=[tok(), tok(), tok(), tok(), tok(), _mod_spec(layer, 1, row_fn), _mod_spec(layer, 2, row_fn),
                  _resident((1, D)), _resident((D, D)), _resident((D, D)),
                  _resident((1, D)), _resident((D, 2 * DFF)), _resident((DFF, D)), _resident((1, D))],
        out_specs=tok(),
        out_shape=jax.ShapeDtypeStruct((n_tok, D), F32),
        compiler_params=_cparams("parallel"),
        name="merge_ffn",
    )(o, z, gd, ma, h, mods, mods, onorm, dproj, wout, nw2, w13, w2, fin)


def _permute_w_in(w_in):
    o_qkv, o_z, o_ab, o_gate, n_in = 2 * D, 5 * D, 6 * D, 6 * D + 4 * H, 6 * D + 4 * H + 2 * D
    pad = jnp.zeros(w_in.shape[:-1] + (AB_PAD - 4 * H,), w_in.dtype)
    return jnp.concatenate([w_in[..., :o_qkv], w_in[..., o_qkv:o_z], w_in[..., o_z:o_ab],
                            w_in[..., o_gate:n_in], w_in[..., o_ab:o_gate], pad], axis=-1)


def _scan_tri(bt, rev):
    r = jnp.arange(bt)[:, None]
    c = jnp.arange(bt)[None, :]
    same = (r // CHUNK) == (c // CHUNK)
    return (same & ((c >= r) if rev else (c <= r))).astype(F32)


def kernel(x, c, ctx, c_ctx, ada_w, ada_b, ffn1_norm, ffn1_w13, ffn1_w2, mix_norm, w_in, conv_dw, conv_dw_b, conv_ln_g, conv_ln_b, conv_proj, dn_short, dn_a_log, dn_dt_bias, dn_onorm, dn_proj, w_out, ffn2_norm, ffn2_w13, ffn2_w2, final_norm):
    bsz, t_lat, _ = x.shape
    t_ctx = ctx.shape[1]
    depth = ada_w.shape[0]
    assert bsz + 1 <= MOD_ROWS and t_lat % DN_BLOCK == 0 and t_ctx % CHUNK == 0 and t_ctx <= DN_BLOCK

    c_all = jnp.concatenate([c, c_ctx[None, :], jnp.zeros((MOD_ROWS - bsz - 1, D), F32)], axis=0)
    mods = _ada(c_all, ada_w, ada_b).reshape(depth, MOD_ROWS, 1, NMOD * D)

    tm = 512
    lat_tiles = t_lat // tm
    lat_row = lambda i: i // lat_tiles
    ctx_row = lambda i: bsz

    w13_1, w2_1 = ffn1_w13.astype(BF16), ffn1_w2.astype(BF16)
    w13_2, w2_2 = ffn2_w13.astype(BF16), ffn2_w2.astype(BF16)
    win = _permute_w_in(w_in).astype(BF16)
    cproj, dproj, wout = conv_proj.astype(BF16), dn_proj.astype(BF16), w_out.astype(BF16)
    row = lambda a: a.reshape(depth, 1, -1)
    n1, nm, n2 = row(ffn1_norm), row(mix_norm), row(ffn2_norm)
    dwb, lng, lnb = row(conv_dw_b), row(conv_ln_g), row(conv_ln_b)
    onorm = row(jnp.tile(dn_onorm, (1, H)))
    lane_pad = jnp.zeros((depth, 1, AB_PAD - 2 * H), F32)
    nega = jnp.concatenate([-jnp.exp(dn_a_log.astype(F32)).reshape(depth, 1, 2 * H), lane_pad], axis=-1)
    dtb = jnp.concatenate([dn_dt_bias.astype(F32).reshape(depth, 1, 2 * H), lane_pad], axis=-1)
    tris = {(t, rev): _scan_tri(min(DN_BLOCK, t), rev) for t in (t_ctx, t_lat) for rev in (False, True)}
    s_zero = jnp.zeros((bsz, H, DK, DK), F32)

    h = x.reshape(bsz * t_lat, D)
    hc = ctx.reshape(bsz * t_ctx, D)
    for l in range(depth):
        last = l == depth - 1
        h = _ffn(h, mods, l, 0, lat_row, n1[l], w13_1[l], w2_1[l], tm)
        hc = _ffn(hc, mods, l, 0, ctx_row, n1[l], w13_1[l], w2_1[l], tm)

        mix_args = (nm[l], win[l], conv_dw[l], dwb[l], lng[l], lnb[l], cproj[l])
        ma_l, qkv_l, z_l, gd_l, ab_l = _mixin(h, mods, l, lat_row, *mix_args, tm=tm, row_len=GRID_W)
        ma_c, qkv_c, z_c, gd_c, ab_c = _mixin(hc, mods, l, ctx_row, *mix_args, tm=tm, row_len=t_ctx,
                                              branches=not last)

        seq = lambda a, t: a.reshape(bsz, t, a.shape[-1])
        dn_args = (dn_short[l], nega[l], dtb[l])
        o_c, s_f, qkvp_c = _deltanet(seq(qkv_c, t_ctx), seq(ab_c, t_ctx), *dn_args, tris[(t_ctx, False)],
                                     s_zero, None, False, False)
        o_c, s_b, _ = _deltanet(qkvp_c, seq(ab_c, t_ctx), *dn_args, tris[(t_ctx, True)],
                                s_zero, o_c, True, True)
        o_l, _, qkvp_l = _deltanet(seq(qkv_l, t_lat), seq(ab_l, t_lat), *dn_args, tris[(t_lat, False)],
                                   s_f, None, False, False)
        o_l, _, _ = _deltanet(qkvp_l, seq(ab_l, t_lat), *dn_args, tris[(t_lat, True)],
                              s_b, o_l, True, True)

        out_args = (onorm[l], dproj[l], wout[l], n2[l], w13_2[l], w2_2[l], final_norm.reshape(1, D))
        h = _merge(o_l.reshape(bsz * t_lat, D), z_l, gd_l, ma_l, h, mods, l, lat_row, *out_args,
                   final=last, tm=tm)
        if not last:
            hc = _merge(o_c.reshape(bsz * t_ctx, D), z_c, gd_c, ma_c, hc, mods, l, ctx_row, *out_args,
                        final=False, tm=tm)
    return h.reshape(bsz, t_lat, D)
```

```python
import functools

import jax
import jax.numpy as jnp
from jax import lax
from jax.experimental import pallas as pl
from jax.experimental.pallas import tpu as pltpu

F32 = jnp.float32
BF16 = jnp.bfloat16

EPS = 1e-6
D = 1024
DFF = 2816
NMOD = 9
GRID_W = 64
CONV_K = 31
SUBLANES = 8
CONV_PAD = 16
SHORT_K = 3
H = 8
DK = 128
CHUNK = 64
DN_BLOCK = 512
HALO = 16
MOD_ROWS = 40

P_CONV = 0
P_QKV = P_CONV + 2 * D
P_Z = P_QKV + 3 * D
P_GATE = P_Z + D
P_AB = P_GATE + 2 * D
AB_PAD = 128
P_END = P_AB + AB_PAD

VMEM_LIMIT = 56 * 1024 * 1024


def _cparams(*sem):
    return pltpu.CompilerParams(dimension_semantics=sem, vmem_limit_bytes=VMEM_LIMIT)


def _resident(shape):
    nd = len(shape)
    return pl.BlockSpec(shape, lambda *_: (0,) * nd, pipeline_mode=pl.Buffered(1))


def _mod_spec(layer, part, row_fn):
    return pl.BlockSpec((None, None, 1, 3 * D), lambda i: (layer, row_fn(i), 0, part))


def _sigmoid(x):
    return jax.nn.sigmoid(x)


def _rms_mod(h, nw, shift, scale):
    ms = jnp.mean(h * h, axis=-1, keepdims=True)
    n = h * lax.rsqrt(ms + EPS) * nw
    return n * (1.0 + scale) + shift


def _ada_kernel(c_ref, w_ref, b_ref, o_ref):
    c = c_ref[...]
    sc = (c * _sigmoid(c)).astype(BF16)
    o_ref[...] = jnp.dot(sc, w_ref[...].astype(BF16), preferred_element_type=F32) + b_ref[...]


def _ada(c_all, ada_w, ada_b):
    depth = ada_w.shape[0]
    rows = c_all.shape[0]
    return pl.pallas_call(
        _ada_kernel,
        grid=(depth, NMOD),
        in_specs=[pl.BlockSpec((rows, D), lambda l, j: (0, 0)),
                  pl.BlockSpec((None, D, D), lambda l, j: (l, 0, j)),
                  pl.BlockSpec((None, 1, D), lambda l, j: (l, 0, j))],
        out_specs=pl.BlockSpec((None, rows, D), lambda l, j: (l, 0, j)),
        out_shape=jax.ShapeDtypeStruct((depth, rows, NMOD * D), F32),
        compiler_params=_cparams("parallel", "parallel"),
        name="ada",
    )(c_all, ada_w, ada_b.reshape(depth, 1, NMOD * D))


def _ffn_step(h, mod, nw, w13_ref, w2_ref):
    n = _rms_mod(h, nw, mod[:, :D], mod[:, D:2 * D])
    ab = jnp.dot(n.astype(BF16), w13_ref[...], preferred_element_type=F32)
    a = ab[:, :DFF]
    b = ab[:, DFF:]
    s = (a * _sigmoid(a) * b).astype(BF16)
    y = jnp.dot(s, w2_ref[...], preferred_element_type=F32)
    return h + (0.5 * mod[:, 2 * D:]) * y


def _ffn_kernel(h_ref, mod_ref, nw_ref, w13_ref, w2_ref, o_ref):
    o_ref[...] = _ffn_step(h_ref[...], mod_ref[...], nw_ref[...], w13_ref, w2_ref)


def _ffn(h, mods, layer, part, row_fn, nw, w13, w2, tm):
    n_tok = h.shape[0]
    return pl.pallas_call(
        _ffn_kernel,
        grid=(n_tok // tm,),
        in_specs=[pl.BlockSpec((tm, D), lambda i: (i, 0)),
                  _mod_spec(layer, part, row_fn),
                  _resident((1, D)),
                  _resident((D, 2 * DFF)),
                  _resident((DFF, D))],
        out_specs=pl.BlockSpec((tm, D), lambda i: (i, 0)),
        out_shape=jax.ShapeDtypeStruct((n_tok, D), F32),
        compiler_params=_cparams("parallel"),
        name="ffn",
    )(h, mods, nw, w13, w2)


def _conv_span(row_len):
    return row_len + SUBLANES * ((CONV_K + SUBLANES - 1) // SUBLANES - 1)


def _mixin_kernel(h_ref, mod_ref, nw_ref, win_ref, *rest, row_len, branches):
    tm = h_ref.shape[0]
    mod = mod_ref[...]
    n = _rms_mod(h_ref[...], nw_ref[...], mod[:, :D], mod[:, D:2 * D]).astype(BF16)
    if not branches:
        qkv_ref, ab_ref = rest
        qkv_ref[...] = jnp.dot(n, win_ref[:, P_QKV:P_Z], preferred_element_type=F32).astype(BF16)
        ab_ref[...] = jnp.dot(n, win_ref[:, P_AB:P_END], preferred_element_type=F32)
        return
    dw_ref, dwb_ref, lng_ref, lnb_ref, cproj_ref, ma_ref, qkv_ref, z_ref, gd_ref, ab_ref, pad_ref, rot_ref = rest
    n_rows = tm // row_len
    stride = row_len + 2 * CONV_PAD

    u = jnp.dot(n, win_ref[:, P_CONV:P_QKV], preferred_element_type=F32)
    y = u[:, :D] * _sigmoid(u[:, D:])
    zeros = jnp.zeros((CONV_PAD, D), F32)
    for r in range(n_rows):
        base = r * stride
        pad_ref[base:base + CONV_PAD, :] = zeros
        pad_ref[base + CONV_PAD:base + CONV_PAD + row_len, :] = y[r * row_len:(r + 1) * row_len, :]
        pad_ref[base + CONV_PAD + row_len:base + stride, :] = zeros

    gates = jnp.dot(n, win_ref[:, P_GATE:P_AB], preferred_element_type=F32)
    g_conv = _sigmoid(gates[:, :D])
    gd_ref[...] = _sigmoid(gates[:, D:]).astype(BF16)
    qkv_ref[...] = jnp.dot(n, win_ref[:, P_QKV:P_Z], preferred_element_type=F32).astype(BF16)
    zz = jnp.dot(n, win_ref[:, P_Z:P_GATE], preferred_element_type=F32)
    z_ref[...] = (zz * _sigmoid(zz)).astype(BF16)
    ab_ref[...] = jnp.dot(n, win_ref[:, P_AB:P_END], preferred_element_type=F32)

    span = _conv_span(row_len)
    first = CONV_PAD - CONV_K // 2
    rows = []
    for r in range(n_rows):
        for s in range(1, SUBLANES):
            rot_ref[s - 1] = pad_ref[r * stride + s:r * stride + s + span, :]
        acc = None
        for k in range(CONV_K):
            a, s = divmod(first + k, SUBLANES)
            if s == 0:
                tap = pad_ref[r * stride + SUBLANES * a:r * stride + SUBLANES * a + row_len, :]
            else:
                tap = rot_ref[s - 1, SUBLANES * a:SUBLANES * a + row_len, :]
            term = dw_ref[k:k + 1, :] * tap
            acc = term if acc is None else acc + term
        rows.append(acc)
    conv = (jnp.concatenate(rows, axis=0) if n_rows > 1 else rows[0]) + dwb_ref[...]
    mu = jnp.mean(conv, axis=-1, keepdims=True)
    cen = conv - mu
    var = jnp.mean(cen * cen, axis=-1, keepdims=True)
    yl = cen * lax.rsqrt(var + EPS) * lng_ref[...] + lnb_ref[...]
    ys = (yl * _sigmoid(yl)).astype(BF16)
    y_conv = jnp.dot(ys, cproj_ref[...], preferred_element_type=F32)
    ma_ref[...] = (g_conv * y_conv).astype(BF16)


def _mixin(h, mods, layer, row_fn, nw, win, dw, dwb, lng, lnb, cproj, tm, row_len, branches=True):
    n_tok = h.shape[0]
    tok = lambda w: pl.BlockSpec((tm, w), lambda i: (i, 0))
    if not branches:
        qkv, ab = pl.pallas_call(
            functools.partial(_mixin_kernel, row_len=row_len, branches=False),
            grid=(n_tok // tm,),
            in_specs=[tok(D), _mod_spec(layer, 1, row_fn), _resident((1, D)), _resident((D, P_END))],
            out_specs=[tok(3 * D), tok(AB_PAD)],
            out_shape=[jax.ShapeDtypeStruct((n_tok, 3 * D), BF16), jax.ShapeDtypeStruct((n_tok, AB_PAD), F32)],
            compiler_params=_cparams("parallel"),
            name="mixin_dn_only",
        )(h, mods, nw, win)
        return None, qkv, None, None, ab
    pad_rows = (tm // row_len) * (row_len + 2 * CONV_PAD)
    return pl.pallas_call(
        functools.partial(_mixin_kernel, row_len=row_len, branches=True),
        grid=(n_tok // tm,),
        in_specs=[tok(D), _mod_spec(layer, 1, row_fn), _resident((1, D)), _resident((D, P_END)),
                  _resident((CONV_K, D)), _resident((1, D)), _resident((1, D)), _resident((1, D)),
                  _resident((D, D))],
        out_specs=[tok(D), tok(3 * D), tok(D), tok(D), tok(AB_PAD)],
        out_shape=[jax.ShapeDtypeStruct((n_tok, D), BF16),
                   jax.ShapeDtypeStruct((n_tok, 3 * D), BF16),
                   jax.ShapeDtypeStruct((n_tok, D), BF16),
                   jax.ShapeDtypeStruct((n_tok, D), BF16),
                   jax.ShapeDtypeStruct((n_tok, AB_PAD), F32)],
        scratch_shapes=[pltpu.VMEM((pad_rows, D), F32),
                        pltpu.VMEM((SUBLANES - 1, _conv_span(row_len), D), F32)],
        compiler_params=_cparams("parallel"),
        name="mixin",
    )(h, mods, nw, win, dw, dwb, lng, lnb, cproj)


def _dot_nt(a, b):
    return lax.dot_general(a, b, (((1,), (1,)), ((), ())), preferred_element_type=F32)


def _dot_tn(a, b):
    return lax.dot_general(a, b, (((0,), (0,)), ((), ())), preferred_element_type=F32)


def _dn_kernel(*refs, rev, add_prev, prepped):
    refs = list(refs)
    if prepped:
        qkvp_ref = refs.pop(0)
    else:
        qkv_ref, qprev_ref, qnext_ref = refs[:3]
        del refs[:3]
    ab_ref = refs.pop(0)
    sw_ref = None if prepped else refs.pop(0)
    nega_ref, dtb_ref, tri_ref, s0_ref = refs[:4]
    del refs[:4]
    oprev_ref = refs.pop(0) if add_prev else None
    o_ref, sfin_ref = refs[:2]
    del refs[:2]
    qkvp_out = None if prepped else refs.pop(0)
    s_ref, q_s, k_s, v_s, gct_s = refs
    bt = ab_ref.shape[0]
    n_chunks = bt // CHUNK
    j = pl.program_id(1)
    nb = pl.num_programs(1)
    blk = (nb - 1 - j) if rev else j

    @pl.when(j == 0)
    def _():
        s_ref[...] = s0_ref[...]

    if prepped:
        q_s[...] = qkvp_ref[:, :D].astype(F32)
        k_s[...] = qkvp_ref[:, D:2 * D].astype(F32)
        v_s[...] = qkvp_ref[:, 2 * D:].astype(F32)
    else:
        x = qkv_ref[...].astype(F32)
        prev_row = jnp.where(blk > 0, qprev_ref[HALO - 1:HALO, :].astype(F32), 0.0)
        next_row = jnp.where(blk < nb - 1, qnext_ref[0:1, :].astype(F32), 0.0)
        rid = lax.broadcasted_iota(jnp.int32, (bt, 1), 0)
        x_prev = jnp.where(rid == 0, prev_row, pltpu.roll(x, 1, 0))
        x_next = jnp.where(rid == bt - 1, next_row, pltpu.roll(x, bt - 1, 0))
        y = sw_ref[0:1, :] * x_prev + sw_ref[1:2, :] * x + sw_ref[2:3, :] * x_next
        y = y * _sigmoid(y)
        for hh in range(H):
            qh = y[:, hh * DK:(hh + 1) * DK]
            kh = y[:, D + hh * DK:D + (hh + 1) * DK]
            q_s[:, hh * DK:(hh + 1) * DK] = qh * (lax.rsqrt(jnp.sum(qh * qh, axis=-1, keepdims=True) + EPS)
                                                  * (DK ** -0.5))
            k_s[:, hh * DK:(hh + 1) * DK] = kh * lax.rsqrt(jnp.sum(kh * kh, axis=-1, keepdims=True) + EPS)
        v_s[...] = y[:, 2 * D:]
        qkvp_out[:, :D] = q_s[...].astype(BF16)
        qkvp_out[:, D:2 * D] = k_s[...].astype(BF16)
        qkvp_out[:, 2 * D:] = y[:, 2 * D:].astype(BF16)

    ab = ab_ref[...]
    sp_in = ab + dtb_ref[...]
    softplus = jnp.maximum(sp_in, 0.0) + jnp.log1p(jnp.exp(-jnp.abs(sp_in)))
    g_all = nega_ref[...] * softplus
    beta_all = _sigmoid(ab)
    gc = jnp.dot(tri_ref[...], g_all, preferred_element_type=F32, precision=lax.Precision.HIGHEST)
    gct_s[...] = gc.T

    d = 1 if rev else 0
    ri = lax.broadcasted_iota(jnp.int32, (CHUNK, CHUNK), 0)
    ci = lax.broadcasted_iota(jnp.int32, (CHUNK, CHUNK), 1)
    incl = (ci >= ri) if rev else (ci <= ri)
    strict = (ci > ri) if rev else (ci < ri)

    units = []
    for step in range(n_chunks):
        c = (n_chunks - 1 - step) if rev else step
        r0 = c * CHUNK
        last = r0 if rev else r0 + CHUNK - 1
        for hh in range(H):
            lane = d * H + hh
            hs = slice(hh * DK, (hh + 1) * DK)
            units.append(dict(step=step, hh=hh, rows=slice(r0, r0 + CHUNK), hs=hs,
                              bcol=beta_all[r0:r0 + CHUNK, 2 * H + lane:2 * H + lane + 1],
                              gcol=gc[r0:r0 + CHUNK, lane:lane + 1],
                              glast=gc[last:last + 1, lane:lane + 1],
                              lane=lane))

    for un in units:
        k = k_s[un["rows"], un["hs"]]
        q = q_s[un["rows"], un["hs"]]
        kb = k * un["bcol"]
        gram = _dot_nt(jnp.concatenate([kb, q], axis=0).astype(BF16), k.astype(BF16))
        grow = gct_s[un["lane"]:un["lane"] + 1, un["rows"]]
        diff = un["gcol"] - grow
        dec = jnp.where(incl, jnp.exp(jnp.where(incl, diff, 0.0)), 0.0)
        un["a"] = jnp.where(strict, gram[:CHUNK] * dec, 0.0)
        un["qk"] = (gram[CHUNK:] * dec).astype(BF16)
        eg = jnp.exp(un["gcol"])
        un["rhs"] = jnp.concatenate([v_s[un["rows"], un["hs"]] * un["bcol"], kb * eg], axis=1)
        un["qe"] = q * eg
        un["kt"] = (k * jnp.exp(un["glast"] - un["gcol"])).T.astype(BF16)
    eye = jnp.where(ri == ci, 1.0, 0.0).astype(F32)
    rc = ri ^ ci
    for un in units:
        un["t"] = eye - jnp.where(rc < 2, un["a"], 0.0)
    size = 2
    while size < CHUNK:
        for un in units:
            off = jnp.where(rc >= size, jnp.where(rc < 2 * size, un["a"], 0.0), 0.0).astype(BF16)
            un["tb"] = un["t"].astype(BF16)
            un["y"] = jnp.dot(un["tb"], off, preferred_element_type=F32).astype(BF16)
        for un in units:
            un["t"] = un["t"] - jnp.dot(un["y"], un["tb"], preferred_element_type=F32)
        size *= 2
    for un in units:
        un["x"] = jnp.dot(un["t"].astype(BF16), un["rhs"].astype(BF16), preferred_element_type=F32)
        un["wq"] = jnp.concatenate([un["x"][:, DK:], un["qe"]], axis=0).astype(BF16)

    for step in range(n_chunks):
        cur = [un for un in units if un["step"] == step]
        for un in cur:
            un["s_old"] = s_ref[un["hh"]]
            un["wqs"] = jnp.dot(un["wq"], un["s_old"].astype(BF16), preferred_element_type=F32)
        for un in cur:
            vb = (un["x"][:, :DK] - un["wqs"][:CHUNK]).astype(BF16)
            o = un["wqs"][CHUNK:] + jnp.dot(un["qk"], vb, preferred_element_type=F32)
            s_ref[un["hh"]] = (un["s_old"] * jnp.exp(un["glast"])
                               + jnp.dot(un["kt"], vb, preferred_element_type=F32))
            if add_prev:
                o = o + oprev_ref[un["rows"], un["hs"]]
            o_ref[un["rows"], un["hs"]] = o

    @pl.when(j == nb - 1)
    def _():
        sfin_ref[...] = s_ref[...]


def _deltanet(qkv, ab, sw, nega, dtb, tri, s0, o_prev, rev, prepped):
    bsz, t, _ = qkv.shape
    bt = min(DN_BLOCK, t)
    nb = t // bt
    hb = bt // HALO
    n_halo = t // HALO
    blk = (lambda j: nb - 1 - j) if rev else (lambda j: j)
    main = lambda w: pl.BlockSpec((None, bt, w), lambda b, j: (b, blk(j), 0))
    const = lambda shape: pl.BlockSpec(shape, lambda b, j: (0,) * len(shape))
    state = pl.BlockSpec((None, H, DK, DK), lambda b, j: (b, 0, 0, 0))
    in_specs, args = [main(3 * D)], [qkv]
    if not prepped:
        in_specs += [pl.BlockSpec((None, HALO, 3 * D), lambda b, j: (b, jnp.maximum(blk(j) * hb - 1, 0), 0)),
                     pl.BlockSpec((None, HALO, 3 * D),
                                  lambda b, j: (b, jnp.minimum((blk(j) + 1) * hb, n_halo - 1), 0))]
        args += [qkv, qkv]
    in_specs.append(main(AB_PAD))
    args.append(ab)
    if not prepped:
        in_specs.append(const((SHORT_K, 3 * D)))
        args.append(sw)
    in_specs += [const((1, AB_PAD)), const((1, AB_PAD)), const((bt, bt)), state]
    args += [nega, dtb, tri, s0]
    if o_prev is not None:
        in_specs.append(main(D))
        args.append(o_prev)
    out_specs = [main(D), state]
    out_shape = [jax.ShapeDtypeStruct((bsz, t, D), F32), jax.ShapeDtypeStruct((bsz, H, DK, DK), F32)]
    if not prepped:
        out_specs.append(main(3 * D))
        out_shape.append(jax.ShapeDtypeStruct((bsz, t, 3 * D), BF16))
    res = pl.pallas_call(
        functools.partial(_dn_kernel, rev=rev, add_prev=o_prev is not None, prepped=prepped),
        grid=(bsz, nb),
        in_specs=in_specs,
        out_specs=out_specs,
        out_shape=out_shape,
        scratch_shapes=[pltpu.VMEM((H, DK, DK), F32),
                        pltpu.VMEM((bt, D), F32), pltpu.VMEM((bt, D), F32), pltpu.VMEM((bt, D), F32),
                        pltpu.VMEM((AB_PAD, bt), F32)],
        compiler_params=_cparams("parallel", "arbitrary"),
        name="deltanet_bwd" if rev else "deltanet_fwd",
    )(*args)
    return res if not prepped else (res[0], res[1], None)


def _merge_kernel(o_ref, z_ref, gd_ref, ma_ref, h_ref, mod_ref, mod2_ref, onorm_ref, dproj_ref, wout_ref,
                  nw2_ref, w13_ref, w2_ref, fin_ref, out_ref, *, final):
    o = o_ref[...]
    parts = []
    for hh in range(H):
        oh = o[:, hh * DK:(hh + 1) * DK]
        parts.append(oh * lax.rsqrt(jnp.mean(oh * oh, axis=-1, keepdims=True) + EPS))
    on = jnp.concatenate(parts, axis=1) * onorm_ref[...] * z_ref[...].astype(F32)
    y_dn = jnp.dot(on.astype(BF16), dproj_ref[...], preferred_element_type=F32)
    m = ma_ref[...].astype(F32) + gd_ref[...].astype(F32) * y_dn
    y = jnp.dot(m.astype(BF16), wout_ref[...], preferred_element_type=F32)
    h = h_ref[...] + mod_ref[...][:, 2 * D:] * y
    h = _ffn_step(h, mod2_ref[...], nw2_ref[...], w13_ref, w2_ref)
    if final:
        h = h * lax.rsqrt(jnp.mean(h * h, axis=-1, keepdims=True) + EPS) * fin_ref[...]
    out_ref[...] = h


def _merge(o, z, gd, ma, h, mods, layer, row_fn, onorm, dproj, wout, nw2, w13, w2, fin, final, tm):
    n_tok = h.shape[0]
    tok = lambda: pl.BlockSpec((tm, D), lambda i: (i, 0))
    return pl.pallas_call(
        functools.partial(_merge_kernel, final=final),
        grid=(n_tok // tm,),
        in_specs=[tok(), tok(), tok(), tok(), tok(), _mod_spec(layer, 1, row_fn), _mod_spec(layer, 2, row_fn),
                  _resident((1, D)), _resident((D, D)), _resident((D, D)),
                  _resident((1, D)), _resident((D, 2 * DFF)), _resident((DFF, D)), _resident((1, D))],
        out_specs=tok(),
        out_shape=jax.ShapeDtypeStruct((n_tok, D), F32),
        compiler_params=_cparams("parallel"),
        name="merge_ffn",
    )(o, z, gd, ma, h, mods, mods, onorm, dproj, wout, nw2, w13, w2, fin)


def _permute_w_in(w_in):
    o_qkv, o_z, o_ab, o_gate, n_in = 2 * D, 5 * D, 6 * D, 6 * D + 4 * H, 6 * D + 4 * H + 2 * D
    pad = jnp.zeros(w_in.shape[:-1] + (AB_PAD - 4 * H,), w_in.dtype)
    return jnp.concatenate([w_in[..., :o_qkv], w_in[..., o_qkv:o_z], w_in[..., o_z:o_ab],
                            w_in[..., o_gate:n_in], w_in[..., o_ab:o_gate], pad], axis=-1)


def _scan_tri(bt, rev):
    r = jnp.arange(bt)[:, None]
    c = jnp.arange(bt)[None, :]
    same = (r // CHUNK) == (c // CHUNK)
    return (same & ((c >= r) if rev else (c <= r))).astype(F32)


def kernel(x, c, ctx, c_ctx, ada_w, ada_b, ffn1_norm, ffn1_w13, ffn1_w2, mix_norm, w_in, conv_dw, conv_dw_b, conv_ln_g, conv_ln_b, conv_proj, dn_short, dn_a_log, dn_dt_bias, dn_onorm, dn_proj, w_out, ffn2_norm, ffn2_w13, ffn2_w2, final_norm):
    bsz, t_lat, _ = x.shape
    t_ctx = ctx.shape[1]
    depth = ada_w.shape[0]
    assert bsz + 1 <= MOD_ROWS and t_lat % DN_BLOCK == 0 and t_ctx % CHUNK == 0 and t_ctx <= DN_BLOCK

    c_all = jnp.concatenate([c, c_ctx[None, :], jnp.zeros((MOD_ROWS - bsz - 1, D), F32)], axis=0)
    mods = _ada(c_all, ada_w, ada_b).reshape(depth, MOD_ROWS, 1, NMOD * D)

    tm = 512
    lat_tiles = t_lat // tm
    lat_row = lambda i: i // lat_tiles
    ctx_row = lambda i: bsz

    w13_1, w2_1 = ffn1_w13.astype(BF16), ffn1_w2.astype(BF16)
    w13_2, w2_2 = ffn2_w13.astype(BF16), ffn2_w2.astype(BF16)
    win = _permute_w_in(w_in).astype(BF16)
    cproj, dproj, wout = conv_proj.astype(BF16), dn_proj.astype(BF16), w_out.astype(BF16)
    row = lambda a: a.reshape(depth, 1, -1)
    n1, nm, n2 = row(ffn1_norm), row(mix_norm), row(ffn2_norm)
    dwb, lng, lnb = row(conv_dw_b), row(conv_ln_g), row(conv_ln_b)
    onorm = row(jnp.tile(dn_onorm, (1, H)))
    lane_pad = jnp.zeros((depth, 1, AB_PAD - 2 * H), F32)
    nega = jnp.concatenate([-jnp.exp(dn_a_log.astype(F32)).reshape(depth, 1, 2 * H), lane_pad], axis=-1)
    dtb = jnp.concatenate([dn_dt_bias.astype(F32).reshape(depth, 1, 2 * H), lane_pad], axis=-1)
    tris = {(t, rev): _scan_tri(min(DN_BLOCK, t), rev) for t in (t_ctx, t_lat) for rev in (False, True)}
    s_zero = jnp.zeros((bsz, H, DK, DK), F32)

    h = x.reshape(bsz * t_lat, D)
    hc = ctx.reshape(bsz * t_ctx, D)
    for l in range(depth):
        last = l == depth - 1
        h = _ffn(h, mods, l, 0, lat_row, n1[l], w13_1[l], w2_1[l], tm)
        hc = _ffn(hc, mods, l, 0, ctx_row, n1[l], w13_1[l], w2_1[l], tm)

        mix_args = (nm[l], win[l], conv_dw[l], dwb[l], lng[l], lnb[l], cproj[l])
        ma_l, qkv_l, z_l, gd_l, ab_l = _mixin(h, mods, l, lat_row, *mix_args, tm=tm, row_len=GRID_W)
        ma_c, qkv_c, z_c, gd_c, ab_c = _mixin(hc, mods, l, ctx_row, *mix_args, tm=tm, row_len=t_ctx,
                                              branches=not last)

        seq = lambda a, t: a.reshape(bsz, t, a.shape[-1])
        dn_args = (dn_short[l], nega[l], dtb[l])
        o_c, s_f, qkvp_c = _deltanet(seq(qkv_c, t_ctx), seq(ab_c, t_ctx), *dn_args, tris[(t_ctx, False)],
                                     s_zero, None, False, False)
        o_c, s_b, _ = _deltanet(qkvp_c, seq(ab_c, t_ctx), *dn_args, tris[(t_ctx, True)],
                                s_zero, o_c, True, True)
        o_l, _, qkvp_l = _deltanet(seq(qkv_l, t_lat), seq(ab_l, t_lat), *dn_args, tris[(t_lat, False)],
                                   s_f, None, False, False)
        o_l, _, _ = _deltanet(qkvp_l, seq(ab_l, t_lat), *dn_args, tris[(t_lat, True)],
                              s_b, o_l, True, True)

        out_args = (onorm[l], dproj[l], wout[l], n2[l], w13_2[l], w2_2[l], final_norm.reshape(1, D))
        h = _merge(o_l.reshape(bsz * t_lat, D), z_l, gd_l, ma_l, h, mods, l, lat_row, *out_args,
                   final=last, tm=tm)
        if not last:
            hc = _merge(o_c.reshape(bsz * t_ctx, D), z_c, gd_c, ma_c, hc, mods, l, ctx_row, *out_args,
                        final=False, tm=tm)
    return h.reshape(bsz, t_lat, D)
```
